```python
import math
import functools
import jax
import jax.numpy as jnp
from jax import lax
import numpy as np

D_MODEL = 1024
BATCH = 16
SEQ = 2048
DEPTH = 2
DEC_BATCH = 128
DEC_SEQ = 8
PAST_LEN = 8192
PAGE_SIZE = 128

F32 = jnp.float32
EPS = 1e-6
A_HEADS = 4
A_HEAD_DIM = D_MODEL // 8
A_WIDTH = A_HEADS * A_HEAD_DIM
CONV_W = 4
DN_CHUNK = 64
B_HEADS = 4
B_NOPE_DIM = D_MODEL // 16
B_ROPE_DIM = D_MODEL // 32
B_V_DIM = D_MODEL // 16
B_WIDTH = B_HEADS * B_V_DIM
B_Q_RANK = 3 * D_MODEL // 16
B_KV_RANK = D_MODEL // 8
ROPE_THETA = 10000.0
MLA_SCALE = (B_NOPE_DIM + B_ROPE_DIM) ** -0.5
C_HEADS = 4
C_V_DIM = D_MODEL // 16
C_QK_DIM = C_V_DIM // 2
C_WIDTH = C_HEADS * C_V_DIM
DIFF_SCALE = C_QK_DIM ** -0.5
MIX_WIDTH = A_WIDTH + B_WIDTH + C_WIDTH
IN_SIZES = (3 * A_WIDTH, A_WIDTH, A_HEADS, A_HEADS,
            B_Q_RANK, B_KV_RANK, B_ROPE_DIM,
            2 * C_HEADS * C_QK_DIM, 2 * C_HEADS * C_QK_DIM, C_WIDTH)
IN_WIDTH = sum(IN_SIZES)
D_FF = 11 * D_MODEL // 4
Q_BLOCK = 128

kernel_name = 'hymba_gdn_mla_diffattn_macaron_step'


def _rmsnorm(x, g):
    xf = x.astype(F32)
    y = xf * lax.rsqrt(jnp.mean(xf * xf, axis=-1, keepdims=True) + EPS)
    return (y * g.astype(F32)).astype(x.dtype)


def _l2norm(x):
    xf = x.astype(F32)
    return xf * lax.rsqrt(jnp.sum(xf * xf, axis=-1, keepdims=True) + EPS)


def _swiglu(x, wg, wu, wd):
    return (jax.nn.silu(x @ wg) * (x @ wu)) @ wd


def _rope(x, pos):
    half = x.shape[-1] // 2
    freqs = jnp.power(ROPE_THETA, -jnp.arange(half, dtype=F32) / half)
    ang = pos[:, None] * freqs[None, :]
    shape = (1, pos.shape[0]) + (1,) * (x.ndim - 3) + (half,)
    cos = jnp.cos(ang).reshape(shape)
    sin = jnp.sin(ang).reshape(shape)
    xf = x.astype(F32)
    x1, x2 = xf[..., :half], xf[..., half:]
    return jnp.concatenate([x1 * cos - x2 * sin, x1 * sin + x2 * cos], axis=-1).astype(x.dtype)


def _split_cols(x, sizes):
    out, start = [], 0
    for s in sizes:
        out.append(x[..., start:start + s])
        start += s
    return out


def _short_conv(x, buf, w):
    T = x.shape[1]
    xx = jnp.concatenate([buf.astype(x.dtype), x], axis=1)
    y = xx[:, 0:T] * w[0]
    for i in range(1, CONV_W):
        y = y + xx[:, i:i + T] * w[i]
    return jax.nn.silu(y), xx[:, T:]


def _gated_delta(q, k, v, g, beta, S0):
    B, T, H, dk = q.shape
    C = min(DN_CHUNK, T)
    pad = (-T) % C
    if pad:
        def padT(a):
            return jnp.pad(a, [(0, 0), (0, pad)] + [(0, 0)] * (a.ndim - 2))
        q, k, v, g, beta = padT(q), padT(k), padT(v), padT(g), padT(beta)
    N = (T + pad) // C

    def chunks(a):
        a = a.reshape((B, N, C, H) + a.shape[3:])
        return jnp.moveaxis(jnp.moveaxis(a, 1, 0), 2, 3)

    q = chunks(q * (dk ** -0.5))
    k = chunks(k)
    v = chunks(v)
    gam = jnp.cumsum(chunks(g), axis=-1)
    bet = chunks(beta)
    incl = jnp.tril(jnp.ones((C, C), bool))
    strict = jnp.tril(jnp.ones((C, C), bool), -1)
    diff = gam[..., :, None] - gam[..., None, :]
    decay = jnp.where(incl, jnp.exp(jnp.where(incl, diff, 0.0)), 0.0)
    kk = jnp.einsum('nbhtd,nbhjd->nbhtj', k, k)
    amat = jnp.where(strict, bet[..., :, None] * kk * decay, 0.0) + jnp.eye(C, dtype=F32)
    rhs = jnp.concatenate([v * bet[..., None], k * (bet * jnp.exp(gam))[..., None]], axis=-1)
    sol = lax.linalg.triangular_solve(amat, rhs, left_side=True, lower=True, unit_diagonal=True)
    dv = v.shape[-1]
    u0, wmat = sol[..., :dv], sol[..., dv:]
    qk = jnp.einsum('nbhtd,nbhjd->nbhtj', q, k) * decay

    def step(S, inp):
        qc, kc, u0c, wc, gc, qkc = inp
        u = u0c - jnp.einsum('bhtk,bhkv->bhtv', wc, S)
        o = jnp.exp(gc)[..., None] * jnp.einsum('bhtk,bhkv->bhtv', qc, S) + jnp.einsum('bhtj,bhjv->bhtv', qkc, u)
        glast = gc[..., -1]
        kd = kc * jnp.exp(glast[..., None] - gc)[..., None]
        S = jnp.exp(glast)[..., None, None] * S + jnp.einsum('bhjk,bhjv->bhkv', kd, u)
        return S, o

    S, o = lax.scan(step, S0, (q, k, u0, wmat, gam, qk))
    o = jnp.moveaxis(jnp.moveaxis(o, 3, 2), 0, 1).reshape(B, N * C, H, dv)
    return o[:, :T], S


def _attn_prompt(q, k, v, scale):
    B, T, Hk, G, dk = q.shape
    dv = v.shape[-1]
    qb_len = min(Q_BLOCK, T)
    nb = T // qb_len
    qb = jnp.moveaxis(q.reshape(B, nb, qb_len, Hk, G, dk), 1, 0)
    kf = k.astype(F32)
    vf = v.astype(F32)
    kpos = jnp.arange(T)

    def block(args):
        qi, i = args
        s = jnp.einsum('bqhgd,bkhd->bhgqk', qi.astype(F32), kf) * scale
        qpos = i * qb_len + jnp.arange(qb_len)
        s = jnp.where(kpos[None, :] <= qpos[:, None], s, -jnp.inf)
        p = jax.nn.softmax(s, axis=-1)
        return jnp.einsum('bhgqk,bkhd->bqhgd', p, vf)

    o = lax.map(block, (qb, jnp.arange(nb)))
    return jnp.moveaxis(o, 0, 1).reshape(B, T, Hk, G, dv)


def _attn_update(carry, q, k, v, scale, mask):
    m, l, acc = carry
    s = jnp.einsum('bqhgd,bkhd->bhgqk', q.astype(F32), k.astype(F32)) * scale
    if mask is not None:
        s = jnp.where(mask, s, -jnp.inf)
    m_new = jnp.maximum(m, jnp.max(s, axis=-1))
    p = jnp.exp(s - m_new[..., None])
    corr = jnp.exp(m - m_new)
    l = l * corr + jnp.sum(p, axis=-1)
    acc = acc * corr[..., None] + jnp.einsum('bhgqk,bkhd->bhgqd', p, v.astype(F32))
    return (m_new, l, acc)


def _attn_paged(q, k_new, v_new, scale, page_fn, page_table):
    B, Tq, Hk, G, _ = q.shape
    dv = v_new.shape[-1]
    init = (jnp.full((B, Hk, G, Tq), -jnp.inf, F32),
            jnp.zeros((B, Hk, G, Tq), F32),
            jnp.zeros((B, Hk, G, Tq, dv), F32))

    def body(carry, phys):
        kp, vp = page_fn(phys)
        return _attn_update(carry, q, kp, vp, scale, None), None

    carry, _ = lax.scan(body, init, page_table.T)
    causal = jnp.tril(jnp.ones((Tq, Tq), bool))
    m, l, acc = _attn_update(carry, q, k_new, v_new, scale, causal)
    return jnp.transpose(acc / l[..., None], (0, 3, 1, 2, 4))


def _mla_page(cache, layer, phys):
    rows = cache[layer, phys]
    return rows[:, :, None, :], rows[:, :, None, :B_KV_RANK]


def _diff_page(cache_k, cache_v, layer, phys):
    k = cache_k[layer, phys]
    v = jnp.repeat(cache_v[layer, phys], 2, axis=2)
    return k, v


def _mix(h, pos, lam_init, p, conv_buf, S0, attn_b, attn_c):
    B, T, _ = h.shape
    (a_qkv, a_z, a_b, a_a, b_cq, b_ckv, b_kr, c_q, c_k, c_v) = _split_cols(h @ p['w_in'], IN_SIZES)
    qkv, new_buf = _short_conv(a_qkv, conv_buf, p['dn_conv_w'])
    aq, ak, av = [t.reshape(B, T, A_HEADS, A_HEAD_DIM) for t in jnp.split(qkv, 3, axis=-1)]
    beta = jax.nn.sigmoid(a_b.astype(F32))
    g = -jnp.exp(p['dn_A_log'].astype(F32)) * jax.nn.softplus(a_a.astype(F32) + p['dn_dt_bias'].astype(F32))
    ao, S = _gated_delta(_l2norm(aq), _l2norm(ak), av.astype(F32), g, beta, S0.astype(F32))
    ao = _rmsnorm(ao, p['dn_g_out']) * jax.nn.silu(a_z.reshape(B, T, A_HEADS, A_HEAD_DIM).astype(F32))
    cq = _rmsnorm(b_cq, p['mla_g_q'])
    qb = jnp.einsum('btr,rhe->bthe', cq, p['mla_w_uq'])
    q_nope = qb[..., :B_NOPE_DIM]
    q_rope = _rope(qb[..., B_NOPE_DIM:], pos)
    ckv = _rmsnorm(b_ckv, p['mla_g_kv'])
    kr = _rope(b_kr, pos)
    mla_row = jnp.concatenate([ckv, kr], axis=-1)
    q_lat = jnp.einsum('bthn,rhn->bthr', q_nope, p['mla_w_uk'])
    q_abs = jnp.concatenate([q_lat, q_rope], axis=-1)[:, :, None]
    o_lat = attn_b(q_abs, mla_row[:, :, None], ckv[:, :, None], MLA_SCALE)[:, :, 0]
    bo = jnp.einsum('bthr,rhe->bthe', o_lat, p['mla_w_uv'].astype(F32))
    cqh = c_q.reshape(B, T, 2 * C_HEADS, 1, C_QK_DIM)
    ckh = c_k.reshape(B, T, 2 * C_HEADS, C_QK_DIM)
    cvh = c_v.reshape(B, T, C_HEADS, C_V_DIM)
    co = attn_c(cqh, ckh, jnp.repeat(cvh, 2, axis=2), DIFF_SCALE).reshape(B, T, C_HEADS, 2, C_V_DIM)
    lam = (jnp.exp(jnp.sum(p['diff_lam_q1'].astype(F32) * p['diff_lam_k1'].astype(F32)))
           - jnp.exp(jnp.sum(p['diff_lam_q2'].astype(F32) * p['diff_lam_k2'].astype(F32))) + lam_init)
    co = _rmsnorm(co[:, :, :, 0] - lam * co[:, :, :, 1], p['diff_g_subln']) * (1.0 - lam_init)
    out = jnp.concatenate([ao.reshape(B, T, A_WIDTH), bo.reshape(B, T, B_WIDTH),
                           co.reshape(B, T, C_WIDTH)], axis=-1).astype(h.dtype)
    return out, (new_buf, S, mla_row, ckh, cvh)


def _layer(x, pos, lam_init, p, conv_buf, S0, attn_b, attn_c):
    h = _rmsnorm(x, p['g_f1_pre'])
    x = x + 0.5 * _rmsnorm(_swiglu(h, p['w_f1_gate'], p['w_f1_up'], p['w_f1_down']), p['g_f1_post'])
    m, st = _mix(_rmsnorm(x, p['g_mix_pre']), pos, lam_init, p, conv_buf, S0, attn_b, attn_c)
    x = x + _rmsnorm(m @ p['w_out'], p['g_mix_post'])
    h = _rmsnorm(x, p['g_f2_pre'])
    x = x + 0.5 * _rmsnorm(_swiglu(h, p['w_f2_gate'], p['w_f2_up'], p['w_f2_down']), p['g_f2_post'])
    return x, st


def setup_inputs(seed: int = 0) -> dict:
    key = jax.random.key(seed)
    kit = iter(jax.random.split(key, 48))
    n_pages = PAST_LEN // PAGE_SIZE
    n_used = DEC_BATCH * n_pages
    n_pool = n_used + n_used // 4

    def nrm(shape, scale):
        return jax.random.normal(next(kit), shape, F32) * scale

    def gain(shape):
        return 1.0 + 0.05 * jax.random.normal(next(kit), shape, F32)

    page_table = jax.random.permutation(next(kit), n_pool)[:n_used].reshape(DEC_BATCH, n_pages).astype(jnp.int32)
    dt = jnp.exp(jax.random.uniform(next(kit), (DEPTH, A_HEADS), F32, math.log(1e-3), math.log(1e-1)))
    return {
        'x_prompt': nrm((BATCH, SEQ, D_MODEL), 1.0),
        'x_sample': nrm((DEC_BATCH, DEC_SEQ, D_MODEL), 1.0),
        'cache_mla': nrm((DEPTH, n_pool, PAGE_SIZE, B_KV_RANK + B_ROPE_DIM), 1.0),
        'cache_diff_k': nrm((DEPTH, n_pool, PAGE_SIZE, 2 * C_HEADS, C_QK_DIM), 1.0),
        'cache_diff_v': nrm((DEPTH, n_pool, PAGE_SIZE, C_HEADS, C_V_DIM), 1.0),
        'state_dn_S': nrm((DEPTH, DEC_BATCH, A_HEADS, A_HEAD_DIM, A_HEAD_DIM), 0.3),
        'state_dn_conv': nrm((DEPTH, DEC_BATCH, CONV_W - 1, 3 * A_WIDTH), 1.0),
        'page_table': page_table,
        'g_f1_pre': gain((DEPTH, D_MODEL)),
        'g_f1_post': gain((DEPTH, D_MODEL)),
        'w_f1_gate': nrm((DEPTH, D_MODEL, D_FF), D_MODEL ** -0.5),
        'w_f1_up': nrm((DEPTH, D_MODEL, D_FF), D_MODEL ** -0.5),
        'w_f1_down': nrm((DEPTH, D_FF, D_MODEL), D_FF ** -0.5),
        'g_mix_pre': gain((DEPTH, D_MODEL)),
        'g_mix_post': gain((DEPTH, D_MODEL)),
        'w_in': nrm((DEPTH, D_MODEL, IN_WIDTH), D_MODEL ** -0.5),
        'w_out': nrm((DEPTH, MIX_WIDTH, D_MODEL), MIX_WIDTH ** -0.5),
        'dn_conv_w': nrm((DEPTH, CONV_W, 3 * A_WIDTH), CONV_W ** -0.5),
        'dn_A_log': jnp.log(jax.random.uniform(next(kit), (DEPTH, A_HEADS), F32, 1.0, 16.0)),
        'dn_dt_bias': dt + jnp.log(-jnp.expm1(-dt)),
        'dn_g_out': gain((DEPTH, A_HEAD_DIM)),
        'mla_g_q': gain((DEPTH, B_Q_RANK)),
        'mla_w_uq': nrm((DEPTH, B_Q_RANK, B_HEADS, B_NOPE_DIM + B_ROPE_DIM), B_Q_RANK ** -0.5),
        'mla_g_kv': gain((DEPTH, B_KV_RANK)),
        'mla_w_uk': nrm((DEPTH, B_KV_RANK, B_HEADS, B_NOPE_DIM), B_KV_RANK ** -0.5),
        'mla_w_uv': nrm((DEPTH, B_KV_RANK, B_HEADS, B_V_DIM), B_KV_RANK ** -0.5),
        'diff_lam_q1': nrm((DEPTH, C_QK_DIM), 0.1),
        'diff_lam_k1': nrm((DEPTH, C_QK_DIM), 0.1),
        'diff_lam_q2': nrm((DEPTH, C_QK_DIM), 0.1),
        'diff_lam_k2': nrm((DEPTH, C_QK_DIM), 0.1),
        'diff_g_subln': gain((DEPTH, C_V_DIM)),
        'g_f2_pre': gain((DEPTH, D_MODEL)),
        'g_f2_post': gain((DEPTH, D_MODEL)),
        'w_f2_gate': nrm((DEPTH, D_MODEL, D_FF), D_MODEL ** -0.5),
        'w_f2_up': nrm((DEPTH, D_MODEL, D_FF), D_MODEL ** -0.5),
        'w_f2_down': nrm((DEPTH, D_FF, D_MODEL), D_FF ** -0.5),
    }


def reference(x_prompt, x_sample, cache_mla, cache_diff_k, cache_diff_v, state_dn_S, state_dn_conv,
              page_table, g_f1_pre, g_f1_post, w_f1_gate, w_f1_up, w_f1_down, g_mix_pre, g_mix_post,
              w_in, w_out, dn_conv_w, dn_A_log, dn_dt_bias, dn_g_out, mla_g_q, mla_w_uq, mla_g_kv,
              mla_w_uk, mla_w_uv, diff_lam_q1, diff_lam_k1, diff_lam_q2, diff_lam_k2, diff_g_subln,
              g_f2_pre, g_f2_post, w_f2_gate, w_f2_up, w_f2_down):
    Bp, Tp, _ = x_prompt.shape
    Ts = x_sample.shape[1]
    past_len = page_table.shape[1] * PAGE_SIZE
    pos_p = jnp.arange(Tp, dtype=F32)
    pos_s = past_len + jnp.arange(Ts, dtype=F32)
    xp, xs = x_prompt, x_sample
    p_st = [[], [], [], [], []]
    s_st = [[], [], [], [], []]
    for l in range(DEPTH):
        lam_init = 0.8 - 0.6 * math.exp(-0.3 * l)
        p = {
            'g_f1_pre': g_f1_pre[l], 'g_f1_post': g_f1_post[l],
            'w_f1_gate': w_f1_gate[l], 'w_f1_up': w_f1_up[l], 'w_f1_down': w_f1_down[l],
            'g_mix_pre': g_mix_pre[l], 'g_mix_post': g_mix_post[l],
            'w_in': w_in[l], 'w_out': w_out[l],
            'dn_conv_w': dn_conv_w[l], 'dn_A_log': dn_A_log[l], 'dn_dt_bias': dn_dt_bias[l],
            'dn_g_out': dn_g_out[l],
            'mla_g_q': mla_g_q[l], 'mla_w_uq': mla_w_uq[l], 'mla_g_kv': mla_g_kv[l],
            'mla_w_uk': mla_w_uk[l], 'mla_w_uv': mla_w_uv[l],
            'diff_lam_q1': diff_lam_q1[l], 'diff_lam_k1': diff_lam_k1[l],
            'diff_lam_q2': diff_lam_q2[l], 'diff_lam_k2': diff_lam_k2[l],
            'diff_g_subln': diff_g_subln[l],
            'g_f2_pre': g_f2_pre[l], 'g_f2_post': g_f2_post[l],
            'w_f2_gate': w_f2_gate[l], 'w_f2_up': w_f2_up[l], 'w_f2_down': w_f2_down[l],
        }
        conv0 = jnp.zeros((Bp, CONV_W - 1, 3 * A_WIDTH), xp.dtype)
        S0 = jnp.zeros((Bp, A_HEADS, A_HEAD_DIM, A_HEAD_DIM), F32)
        xp, st = _layer(xp, pos_p, lam_init, p, conv0, S0, _attn_prompt, _attn_prompt)
        for i in range(5):
            p_st[i].append(st[i])
        attn_b = functools.partial(_attn_paged, page_fn=functools.partial(_mla_page, cache_mla, l),
                                   page_table=page_table)
        attn_c = functools.partial(_attn_paged,
                                   page_fn=functools.partial(_diff_page, cache_diff_k, cache_diff_v, l),
                                   page_table=page_table)
        xs, st = _layer(xs, pos_s, lam_init, p, state_dn_conv[l], state_dn_S[l], attn_b, attn_c)
        for i in range(5):
            s_st[i].append(st[i])
    p_conv, p_S, p_mla, p_dk, p_dv = [jnp.stack(a, axis=0) for a in p_st]
    s_conv, s_S, s_mla, s_dk, s_dv = [jnp.stack(a, axis=0) for a in s_st]
    return (xp, xs, p_mla, p_dk, p_dv, p_S, p_conv, s_mla, s_dk, s_dv, s_S, s_conv)
```

```python
import functools
import math

import jax
import jax.numpy as jnp
import numpy as np
from jax import lax
from jax.experimental import pallas as pl
from jax.experimental.pallas import tpu as pltpu

F32 = jnp.float32
BF16 = jnp.bfloat16
EPS = 1e-6
ROPE_THETA = 10000.0
CONV_W = 4
DN_CHUNK = 64
PAGE = 128
LANES = 128
VMEM_LIMIT = 52 * 1024 * 1024
NEG = -1e30

_P_QKV, _P_Z, _P_BA, _P_CQ, _P_CKV, _P_KR, _P_KRS, _P_DQ, _P_DK, _P_DV, _P_END = (
    0, 1536, 2048, 2176, 2432, 2560, 2688, 2816, 3072, 3328, 3840)


def _cp(sem):
    return pltpu.CompilerParams(dimension_semantics=sem, vmem_limit_bytes=VMEM_LIMIT)


def _tile(m, pref):
    t = pref
    while m % t:
        t //= 2
    return t


def _mm(a, b):
    return jnp.dot(a.astype(BF16), b.astype(BF16), preferred_element_type=F32)


def _mm_nt(a, b):
    return lax.dot_general(a.astype(BF16), b.astype(BF16), (((1,), (1,)), ((), ())),
                           preferred_element_type=F32)


def _mm_tn(a, b):
    return lax.dot_general(a.astype(BF16), b.astype(BF16), (((0,), (0,)), ((), ())),
                           preferred_element_type=F32)


def _rms(x, g, n=None):
    n = x.shape[-1] if n is None else n
    ms = jnp.sum(x * x, axis=-1, keepdims=True) * (1.0 / n)
    return x * lax.rsqrt(ms + EPS) * g


def _silu(x):
    return x / (1.0 + jnp.exp(-x))


def _lane_group(lane, width, idx):
    return (lane >= idx * width) & (lane < (idx + 1) * width)


def _ffn_kernel(x_ref, gpre_ref, wg_ref, wu_ref, wd_ref, gpost_ref, o_ref, h_scr, acc_scr):
    f = pl.program_id(1)

    @pl.when(f == 0)
    def _():
        h_scr[...] = _rms(x_ref[...], gpre_ref[...]).astype(BF16)

    h = h_scr[...]
    a = jnp.dot(h, wg_ref[...], preferred_element_type=F32)
    b = jnp.dot(h, wu_ref[...], preferred_element_type=F32)
    act = (_silu(a) * b).astype(BF16)
    part = jnp.dot(act, wd_ref[...], preferred_element_type=F32)

    @pl.when(f == 0)
    def _():
        acc_scr[...] = part

    @pl.when(f != 0)
    def _():
        acc_scr[...] += part

    @pl.when(f == pl.num_programs(1) - 1)
    def _():
        o_ref[...] = x_ref[...] + 0.5 * _rms(acc_scr[...], gpost_ref[...])


def _ffn(x, gpre, wg, wu, wd, gpost):
    m, d = x.shape
    ff = wg.shape[1]
    tm = _tile(m, 512)
    tf = ff // 2 if (ff // 2) % LANES == 0 else ff
    return pl.pallas_call(
        _ffn_kernel,
        out_shape=jax.ShapeDtypeStruct((m, d), F32),
        grid=(m // tm, ff // tf),
        in_specs=[
            pl.BlockSpec((tm, d), lambda i, f: (i, 0)),
            pl.BlockSpec((1, d), lambda i, f: (0, 0)),
            pl.BlockSpec((d, tf), lambda i, f: (0, f)),
            pl.BlockSpec((d, tf), lambda i, f: (0, f)),
            pl.BlockSpec((tf, d), lambda i, f: (f, 0)),
            pl.BlockSpec((1, d), lambda i, f: (0, 0)),
        ],
        out_specs=pl.BlockSpec((tm, d), lambda i, f: (i, 0)),
        scratch_shapes=[pltpu.VMEM((tm, d), BF16), pltpu.VMEM((tm, d), F32)],
        compiler_params=_cp(("parallel", "arbitrary")),
        name="ffn",
    )(x, gpre, wg, wu, wd, gpost)


def _proj_kernel(x_ref, g_ref, w_ref, cos_ref, sin_ref, gq_ref, wuq_ref, wuk_ref, gkv_ref,
                 qkv_ref, z_ref, ba_ref, row_ref, kp_ref, qlat_ref, qrope_ref,
                 dq_ref, dk_ref, dv_ref, dkb_ref, dvb_ref):
    h = _rms(x_ref[...], g_ref[...]).astype(BF16)

    def proj(lo, hi):
        return jnp.dot(h, w_ref[:, lo:hi], preferred_element_type=F32)

    qkv_ref[...] = proj(_P_QKV, _P_Z)
    z_ref[...] = proj(_P_Z, _P_BA)
    ba_ref[...] = proj(_P_BA, _P_CQ)
    cos = cos_ref[...]
    sin = sin_ref[...]
    cq = _rms(proj(_P_CQ, _P_CKV), gq_ref[...], n=192)
    qb = _mm(cq, wuq_ref[...])
    qlat_ref[...] = _mm(qb[:, 0:256], wuk_ref[...]).astype(BF16)
    qrope_ref[...] = (qb[:, 256:384] * cos + qb[:, 384:512] * sin).astype(BF16)
    ckv = _rms(proj(_P_CKV, _P_KR), gkv_ref[...])
    kr = proj(_P_KR, _P_KRS) * cos + proj(_P_KRS, _P_DQ) * sin
    row_ref[:, 0:128] = ckv
    row_ref[:, 128:160] = kr[:, 0:32]
    kp_ref[:, 0:128] = ckv.astype(BF16)
    kp_ref[:, 128:256] = kr.astype(BF16)
    dq_ref[...] = proj(_P_DQ, _P_DK).astype(BF16)
    dk = proj(_P_DK, _P_DV)
    dk_ref[...] = dk
    dkb_ref[...] = dk.astype(BF16)
    dvd = proj(_P_DV, _P_END)
    dvb_ref[...] = dvd.astype(BF16)
    lane = lax.broadcasted_iota(jnp.int32, (dvd.shape[0], LANES), 1)
    for j in range(2):
        dv_ref[:, j * 128:(j + 1) * 128] = jnp.where(
            lane < 64, dvd[:, (2 * j) * 128:(2 * j + 1) * 128],
            dvd[:, (2 * j + 1) * 128:(2 * j + 2) * 128])


def _proj(x, g, w, cos, sin, gq, wuq, wuk, gkv):
    m, d = x.shape
    tm = _tile(m, 512)
    row = lambda n: pl.BlockSpec((tm, n), lambda i: (i, 0))
    full = lambda a: pl.BlockSpec(a.shape, lambda i: (0, 0))
    outs = [(1536, F32), (512, F32), (128, F32), (160, F32), (256, BF16), (512, BF16),
            (128, BF16), (256, BF16), (256, F32), (256, F32), (256, BF16), (512, BF16)]
    return pl.pallas_call(
        _proj_kernel,
        out_shape=[jax.ShapeDtypeStruct((m, n), dt) for n, dt in outs],
        grid=(m // tm,),
        in_specs=[row(d), full(g), full(w), row(128), row(128), full(gq), full(wuq),
                  full(wuk), full(gkv)],
        out_specs=[row(n) for n, _ in outs],
        compiler_params=_cp(("parallel",)),
        name="proj_in",
    )(x, g, w, cos, sin, gq, wuq, wuk, gkv)


def _gdn_kernel(x_ref, z_ref, ba_ref, conv0_ref, s0_ref, cw_ref, alog_ref, dtb_ref, gout_ref,
                o_ref, sout_ref, xs_scr, s_scr, *, t_in, t_cmp, t_valid):
    C = DN_CHUNK
    tb = pl.program_id(1)

    @pl.when(tb == 0)
    def _():
        s_scr[...] = s0_ref[...]
        xs_scr[5:8, :] = conv0_ref[...]

    @pl.when(tb != 0)
    def _():
        xs_scr[5:8, :] = xs_scr[5 + t_in:8 + t_in, :]

    xs_scr[8:8 + t_in, :] = x_ref[...]
    if t_cmp > t_in:
        xs_scr[8 + t_in:8 + t_cmp, :] = jnp.zeros((t_cmp - t_in, xs_scr.shape[1]), F32)

    ri = lax.broadcasted_iota(jnp.int32, (C, C), 0)
    ci = lax.broadcasted_iota(jnp.int32, (C, C), 1)
    incl = ri >= ci
    strict = ri > ci
    tril = jnp.where(incl, 1.0, 0.0).astype(F32)
    alog = alog_ref[...]
    dtb = dtb_ref[...]
    gout = gout_ref[...]

    for c in range(t_cmp // C):
        r0 = c * C
        y = xs_scr[5 + r0:5 + r0 + C, :] * cw_ref[0:1, :]
        for i in range(1, CONV_W):
            y = y + xs_scr[5 + r0 + i:5 + r0 + i + C, :] * cw_ref[i:i + 1, :]
        qkv = _silu(y)
        ba = ba_ref[r0:r0 + C, :] if t_in >= t_cmp else None
        if ba is None:
            ba = jnp.concatenate([ba_ref[...], jnp.zeros((t_cmp - t_in, LANES), F32)], axis=0)
        bet_all = 1.0 / (1.0 + jnp.exp(-ba))
        xa = ba + dtb
        g_all = -jnp.exp(alog) * (jnp.maximum(xa, 0.0) + jnp.log(1.0 + jnp.exp(-jnp.abs(xa))))
        if t_valid < t_cmp:
            valid = lax.broadcasted_iota(jnp.int32, (C, 1), 0) + r0 < t_valid
            qkv = jnp.where(valid, qkv, 0.0)
            bet_all = jnp.where(valid, bet_all, 0.0)
            g_all = jnp.where(valid, g_all, 0.0)
        gam_all = jnp.dot(tril, g_all, preferred_element_type=F32,
                          precision=lax.Precision.HIGHEST)
        gam_t = gam_all.T
        for h in range(4):
            q = qkv[:, h * 128:(h + 1) * 128]
            k = qkv[:, 512 + h * 128:512 + (h + 1) * 128]
            v = qkv[:, 1024 + h * 128:1024 + (h + 1) * 128]
            q = q * lax.rsqrt(jnp.sum(q * q, axis=-1, keepdims=True) + EPS) * (128.0 ** -0.5)
            k = k * lax.rsqrt(jnp.sum(k * k, axis=-1, keepdims=True) + EPS)
            bet = bet_all[:, h:h + 1]
            gcol = gam_all[:, 4 + h:5 + h]
            grow = gam_t[4 + h:5 + h, :]
            glast = gam_all[C - 1:C, 4 + h:5 + h]
            diff = gcol - grow
            decay = jnp.where(incl, jnp.exp(jnp.where(incl, diff, 0.0)), 0.0)
            kk = _mm_nt(k, k)
            qk = _mm_nt(q, k) * decay
            tm = jnp.where(strict, bet * kk * decay, 0.0)
            eg = jnp.exp(gcol)
            x = jnp.concatenate([v * bet, k * (bet * eg)], axis=-1)
            x = x - _mm(tm, x)
            tp = tm
            for _ in range(5):
                tp = _mm(tp, tp)
                x = x + _mm(tp, x)
            u0 = x[:, 0:128]
            w = x[:, 128:256]
            s_h = s_scr[0, h]
            u = u0 - _mm(w, s_h)
            o = eg * _mm(q, s_h) + _mm(qk, u)
            kd = k * jnp.exp(glast - gcol)
            s_scr[0, h] = jnp.exp(glast) * s_h + _mm_tn(kd, u)
            if r0 < t_in:
                zz = z_ref[r0:r0 + min(C, t_in), h * 128:(h + 1) * 128]
                oo = o[0:min(C, t_in), :]
                o_ref[r0:r0 + min(C, t_in), h * 128:(h + 1) * 128] = (
                    _rms(oo, gout) * _silu(zz)).astype(o_ref.dtype)

    @pl.when(tb == pl.num_programs(1) - 1)
    def _():
        sout_ref[...] = s_scr[...]


def _gdn(qkv, z, ba, conv0, s0, layer, cw, alog, dtb, gout, *, nb, t):
    if t >= 128:
        t_in = t_cmp = 128
        t_valid = 128
    else:
        t_in, t_cmp, t_valid = t, DN_CHUNK, t
    nt = t // t_in
    rows = lambda n: pl.BlockSpec((t_in, n), lambda b, j: (b * nt + j, 0))
    full = lambda a: pl.BlockSpec(a.shape, lambda b, j: (0,) * a.ndim)
    kern = functools.partial(_gdn_kernel, t_in=t_in, t_cmp=t_cmp, t_valid=t_valid)
    return pl.pallas_call(
        kern,
        out_shape=[jax.ShapeDtypeStruct((nb * t, 512), BF16 if t_in % 16 == 0 else F32),
                   jax.ShapeDtypeStruct((nb, 4, 128, 128), F32)],
        grid=(nb, nt),
        in_specs=[rows(1536), rows(512), rows(128),
                  pl.BlockSpec((None, None, 3, 1536), lambda b, j: (layer, b, 0, 0)),
                  pl.BlockSpec((None, 1, 4, 128, 128), lambda b, j: (layer, b, 0, 0, 0)),
                  full(cw), full(alog), full(dtb), full(gout)],
        out_specs=[rows(512), pl.BlockSpec((1, 4, 128, 128), lambda b, j: (b, 0, 0, 0))],
        scratch_shapes=[pltpu.VMEM((8 + t_cmp, 1536), F32), pltpu.VMEM((1, 4, 128, 128), F32)],
        compiler_params=_cp(("parallel", "arbitrary")),
        name="gdn",
    )(qkv, z, ba, conv0, s0, cw, alog, dtb, gout)


def _softmax_step(s, v, m_scr, l_scr, acc_scr):
    m_old = m_scr[...]
    m_new = jnp.maximum(m_old, jnp.max(s, axis=-1, keepdims=True))
    p = jnp.exp(s - m_new)
    corr = jnp.exp(m_old - m_new)
    l_scr[...] = l_scr[...] * corr + jnp.sum(p, axis=-1, keepdims=True)
    acc_scr[...] = acc_scr[...] * corr + jnp.dot(p.astype(BF16), v, preferred_element_type=F32)
    m_scr[...] = m_new


def _causal_sweep(i, tq, tk, nheads, step):
    n_full = (i * tq) // tk
    n_tot = ((i + 1) * tq + tk - 1) // tk

    def full_body(j, carry):
        step(j, None)
        return carry

    lax.fori_loop(0, n_full, full_body, 0)

    def diag_body(j, carry):
        qpos = i * tq + lax.broadcasted_iota(jnp.int32, (tq, tk), 0)
        kpos = j * tk + lax.broadcasted_iota(jnp.int32, (tq, tk), 1)
        mask = kpos <= qpos
        step(j, jnp.concatenate([mask] * nheads, axis=0))
        return carry

    lax.fori_loop(n_full, n_tot, diag_body, 0)


def _mla_attn_kernel(qlat_ref, qrope_ref, kp_ref, wuv_ref, o_ref, q_scr, m_scr, l_scr, acc_scr,
                     *, tq, tk, scale):
    i = pl.program_id(1)
    qr = qrope_ref[...]
    lane = lax.broadcasted_iota(jnp.int32, qr.shape, 1)
    for h in range(4):
        q_scr[h * tq:(h + 1) * tq, 0:128] = qlat_ref[:, h * 128:(h + 1) * 128]
        q_scr[h * tq:(h + 1) * tq, 128:256] = jnp.where(_lane_group(lane, 32, h), qr,
                                                        jnp.zeros_like(qr))
    m_scr[...] = jnp.full(m_scr.shape, -jnp.inf, F32)
    l_scr[...] = jnp.zeros(l_scr.shape, F32)
    acc_scr[...] = jnp.zeros(acc_scr.shape, F32)

    def step(j, mask):
        kc = kp_ref[pl.ds(pl.multiple_of(j * tk, tk), tk), :]
        s = lax.dot_general(q_scr[...], kc, (((1,), (1,)), ((), ())),
                            preferred_element_type=F32) * scale
        if mask is not None:
            s = jnp.where(mask, s, NEG)
        _softmax_step(s, kc[:, 0:128], m_scr, l_scr, acc_scr)

    _causal_sweep(i, tq, tk, 4, step)
    o = (acc_scr[...] / l_scr[...]).astype(BF16)
    bo = jnp.dot(o[0:tq], wuv_ref[0:128, :], preferred_element_type=F32)
    for h in range(1, 4):
        bo = bo + jnp.dot(o[h * tq:(h + 1) * tq], wuv_ref[h * 128:(h + 1) * 128, :],
                          preferred_element_type=F32)
    o_ref[...] = bo.astype(BF16)


def _mla_attn(qlat, qrope, kp, wuv, *, nb, t, scale):
    tq = tk = _tile(t, 256)
    nq = t // tq
    kern = functools.partial(_mla_attn_kernel, tq=tq, tk=tk, scale=scale)
    return pl.pallas_call(
        kern,
        out_shape=jax.ShapeDtypeStruct((nb * t, 256), BF16),
        grid=(nb, nq),
        in_specs=[pl.BlockSpec((tq, 512), lambda b, i: (b * nq + i, 0)),
                  pl.BlockSpec((tq, 128), lambda b, i: (b * nq + i, 0)),
                  pl.BlockSpec((t, 256), lambda b, i: (b, 0)),
                  pl.BlockSpec(wuv.shape, lambda b, i: (0, 0))],
        out_specs=pl.BlockSpec((tq, 256), lambda b, i: (b * nq + i, 0)),
        scratch_shapes=[pltpu.VMEM((4 * tq, 256), BF16), pltpu.VMEM((4 * tq, 1), F32),
                        pltpu.VMEM((4 * tq, 1), F32), pltpu.VMEM((4 * tq, 128), F32)],
        compiler_params=_cp(("parallel", "parallel")),
        name="mla_attn",
    )(qlat, qrope, kp, wuv)


def _diff_lambda(lq1, lk1, lq2, lk2, lam_init):
    return (jnp.exp(jnp.sum(lq1 * lk1, axis=-1, keepdims=True))
            - jnp.exp(jnp.sum(lq2 * lk2, axis=-1, keepdims=True)) + lam_init)


def _diff_attn_kernel(q_ref, k_ref, v_ref, lq1_ref, lk1_ref, lq2_ref, lk2_ref, gsub_ref, o_ref,
                      q_scr, m_scr, l_scr, acc_scr, *, tq, tk, scale, lam_init):
    i = pl.program_id(1)
    q = q_ref[...]
    lane = lax.broadcasted_iota(jnp.int32, q.shape, 1)
    for h in range(8):
        q_scr[h * tq:(h + 1) * tq, :] = jnp.where(_lane_group(lane, 32, h), q, jnp.zeros_like(q))
    m_scr[...] = jnp.full(m_scr.shape, -jnp.inf, F32)
    l_scr[...] = jnp.zeros(l_scr.shape, F32)
    acc_scr[...] = jnp.zeros(acc_scr.shape, F32)

    def step(j, mask):
        start = pl.multiple_of(j * tk, tk)
        kc = k_ref[pl.ds(start, tk), :]
        s = lax.dot_general(q_scr[...], kc, (((1,), (1,)), ((), ())),
                            preferred_element_type=F32) * scale
        if mask is not None:
            s = jnp.where(mask, s, NEG)
        m_old = m_scr[...]
        m_new = jnp.maximum(m_old, jnp.max(s, axis=-1, keepdims=True))
        p = jnp.exp(s - m_new)
        corr = jnp.exp(m_old - m_new)
        l_scr[...] = l_scr[...] * corr + jnp.sum(p, axis=-1, keepdims=True)
        p = p.astype(BF16)
        m_scr[...] = m_new
        for c in range(4):
            vc = v_ref[pl.ds(start, tk), c * 128:(c + 1) * 128]
            r = slice(2 * c * tq, (2 * c + 2) * tq)
            acc_scr[r, :] = acc_scr[r, :] * corr[r, :] + jnp.dot(
                p[r, :], vc, preferred_element_type=F32)

    _causal_sweep(i, tq, tk, 8, step)
    lam = _diff_lambda(lq1_ref[...], lk1_ref[...], lq2_ref[...], lk2_ref[...], lam_init)
    on = acc_scr[...] / l_scr[...]
    lane128 = lax.broadcasted_iota(jnp.int32, (tq, LANES), 1)
    outs = []
    for c in range(4):
        d = on[2 * c * tq:(2 * c + 1) * tq] - lam * on[(2 * c + 1) * tq:(2 * c + 2) * tq]
        outs.append(_rms(d, gsub_ref[...]) * (1.0 - lam_init))
    for j in range(2):
        o_ref[:, j * 128:(j + 1) * 128] = jnp.where(
            lane128 < 64, outs[2 * j], outs[2 * j + 1]).astype(BF16)


def _diff_attn(q, k, vdup, lq1, lk1, lq2, lk2, gsub, *, nb, t, scale, lam_init):
    tq = _tile(t, 128)
    tk = _tile(t, 256)
    nq = t // tq
    kern = functools.partial(_diff_attn_kernel, tq=tq, tk=tk, scale=scale, lam_init=lam_init)
    vec = lambda a: pl.BlockSpec(a.shape, lambda b, i: (0, 0))
    return pl.pallas_call(
        kern,
        out_shape=jax.ShapeDtypeStruct((nb * t, 256), BF16),
        grid=(nb, nq),
        in_specs=[pl.BlockSpec((tq, 256), lambda b, i: (b * nq + i, 0)),
                  pl.BlockSpec((t, 256), lambda b, i: (b, 0)),
                  pl.BlockSpec((t, 512), lambda b, i: (b, 0)),
                  vec(lq1), vec(lk1), vec(lq2), vec(lk2), vec(gsub)],
        out_specs=pl.BlockSpec((tq, 256), lambda b, i: (b * nq + i, 0)),
        scratch_shapes=[pltpu.VMEM((8 * tq, 256), BF16), pltpu.VMEM((8 * tq, 1), F32),
                        pltpu.VMEM((8 * tq, 1), F32), pltpu.VMEM((8 * tq, 128), F32)],
        compiler_params=_cp(("parallel", "parallel")),
        name="diff_attn",
    )(q, k, vdup, lq1, lk1, lq2, lk2, gsub)


def _paged_kernel(pt_ref, *refs, g, ts, mla_scale, diff_scale, lam_init):
    mla_pages = refs[0:g]
    dk_pages = refs[g:2 * g]
    dv_pages = refs[2 * g:3 * g]
    (qlat_ref, qr_ref, dq_ref, row_ref, dkn_ref, dvn_ref, wuv_ref, lq1_ref, lk1_ref, lq2_ref,
     lk2_ref, gsub_ref, bo_ref, co_ref, dqb_scr, ma_scr, la_scr, aa_scr, md_scr, ld_scr,
     ad_scr) = refs[3 * g:]
    del pt_ref
    j = pl.program_id(1)
    nh = 4 * ts
    nd = 8 * ts

    @pl.when(j == 0)
    def _():
        q = dq_ref[...]
        lane = lax.broadcasted_iota(jnp.int32, q.shape, 1)
        for h in range(8):
            dqb_scr[h * ts:(h + 1) * ts, :] = jnp.where(_lane_group(lane, 32, h), q, 0.0)
        ma_scr[...] = jnp.full(ma_scr.shape, -jnp.inf, F32)
        la_scr[...] = jnp.zeros(la_scr.shape, F32)
        aa_scr[...] = jnp.zeros(aa_scr.shape, F32)
        md_scr[...] = jnp.full(md_scr.shape, -jnp.inf, F32)
        ld_scr[...] = jnp.zeros(ld_scr.shape, F32)
        ad_scr[...] = jnp.zeros(ad_scr.shape, F32)

    qlat = qlat_ref[...].astype(BF16)
    qr = qr_ref[...].astype(BF16)
    dqb = dqb_scr[...].astype(BF16)

    def update(s_list, v_list, m_scr, l_scr, acc_scr):
        s = jnp.concatenate(s_list, axis=-1) if len(s_list) > 1 else s_list[0]
        m_old = m_scr[...]
        m_new = jnp.maximum(m_old, jnp.max(s, axis=-1, keepdims=True))
        p = jnp.exp(s - m_new)
        corr = jnp.exp(m_old - m_new)
        l_scr[...] = l_scr[...] * corr + jnp.sum(p, axis=-1, keepdims=True)
        pb = p.astype(BF16)
        acc = acc_scr[...] * corr
        off = 0
        for v in v_list:
            n = v.shape[0]
            acc = acc + jnp.dot(pb[:, off:off + n], v, preferred_element_type=F32)
            off += n
        acc_scr[...] = acc
        m_scr[...] = m_new

    def mla_scores(rows):
        ka = rows[:, 0:128].astype(BF16)
        kb = rows[:, 128:160].astype(BF16)
        s = (lax.dot_general(qlat, ka, (((1,), (1,)), ((), ())), preferred_element_type=F32)
             + lax.dot_general(qr, kb, (((1,), (1,)), ((), ())), preferred_element_type=F32))
        return s * mla_scale, ka

    def diff_scores(krows):
        return lax.dot_general(dqb, krows.astype(BF16), (((1,), (1,)), ((), ())),
                               preferred_element_type=F32) * diff_scale

    sa, va = zip(*[mla_scores(r[...]) for r in mla_pages])
    update(list(sa), list(va), ma_scr, la_scr, aa_scr)
    sd = [diff_scores(r[...]) for r in dk_pages]
    update(sd, [r[...].astype(BF16) for r in dv_pages], md_scr, ld_scr, ad_scr)

    @pl.when(j == pl.num_programs(1) - 1)
    def _():
        tsp = -(-ts // 16) * 16
        qpos = lax.broadcasted_iota(jnp.int32, (ts, tsp), 0)
        kpos = lax.broadcasted_iota(jnp.int32, (ts, tsp), 1)
        mask = kpos <= qpos

        def padded(ref):
            rows = ref[...]
            return jnp.concatenate([rows, jnp.zeros((tsp - ts, rows.shape[1]), F32)], axis=0)

        s, v = mla_scores(padded(row_ref))
        update([jnp.where(jnp.concatenate([mask] * 4, axis=0), s, NEG)], [v],
               ma_scr, la_scr, aa_scr)
        s = diff_scores(padded(dkn_ref))
        update([jnp.where(jnp.concatenate([mask] * 8, axis=0), s, NEG)],
               [padded(dvn_ref).astype(BF16)], md_scr, ld_scr, ad_scr)
        o = (aa_scr[...] / la_scr[...]).astype(BF16)
        bo = jnp.zeros((ts, 256), F32)
        for h in range(4):
            full = jnp.dot(o, wuv_ref[h * 128:(h + 1) * 128, :], preferred_element_type=F32)
            bo = bo + full[h * ts:(h + 1) * ts]
        bo_ref[...] = bo
        on = ad_scr[...] / ld_scr[...]
        lam = _diff_lambda(lq1_ref[...], lk1_ref[...], lq2_ref[...], lk2_ref[...], lam_init)
        lane = lax.broadcasted_iota(jnp.int32, (ts, 256), 1)
        d = jnp.zeros((ts, 256), F32)
        for c in range(4):
            dc = on[2 * c * ts:(2 * c + 1) * ts] - lam * on[(2 * c + 1) * ts:(2 * c + 2) * ts]
            d = jnp.where(_lane_group(lane, 64, c), dc, d)
        ms = jnp.zeros((ts, 256), F32)
        for c in range(4):
            sel = _lane_group(lane, 64, c)
            ssq = jnp.sum(jnp.where(sel, d * d, 0.0), axis=-1, keepdims=True) * (1.0 / 64)
            ms = jnp.where(sel, ssq, ms)
        co_ref[...] = d * lax.rsqrt(ms + EPS) * gsub_ref[...] * (1.0 - lam_init)


def _paged(page_table, cache_mla, cache_dk, cache_dv, layer, qlat, qr, dq, row, dkn, dvn, wuv,
           lq1, lk1, lq2, lk2, gsub4, *, nb, ts, mla_scale, diff_scale, lam_init):
    n_pages = page_table.shape[1]
    g = _tile(n_pages, 8)
    kern = functools.partial(_paged_kernel, g=g, ts=ts, mla_scale=mla_scale,
                             diff_scale=diff_scale, lam_init=lam_init)

    def page_spec(width, gi):
        return pl.BlockSpec((None, None, PAGE, width),
                            lambda b, j, pt: (layer, pt[b, j * g + gi], 0, 0))

    per_b = lambda r, n: pl.BlockSpec((None, r, n), lambda b, j, pt: (b, 0, 0))
    vec = lambda a: pl.BlockSpec(a.shape, lambda b, j, pt: (0, 0))
    in_specs = ([page_spec(160, gi) for gi in range(g)] + [page_spec(256, gi) for gi in range(g)]
                + [page_spec(256, gi) for gi in range(g)]
                + [per_b(4 * ts, 128), per_b(4 * ts, 32), per_b(ts, 256), per_b(ts, 160),
                   per_b(ts, 256), per_b(ts, 256), vec(wuv), vec(lq1), vec(lk1), vec(lq2),
                   vec(lk2), vec(gsub4)])
    grid_spec = pltpu.PrefetchScalarGridSpec(
        num_scalar_prefetch=1,
        grid=(nb, n_pages // g),
        in_specs=in_specs,
        out_specs=[per_b(ts, 256), per_b(ts, 256)],
        scratch_shapes=[pltpu.VMEM((8 * ts, 256), F32),
                        pltpu.VMEM((4 * ts, 1), F32), pltpu.VMEM((4 * ts, 1), F32),
                        pltpu.VMEM((4 * ts, 128), F32),
                        pltpu.VMEM((8 * ts, 1), F32), pltpu.VMEM((8 * ts, 1), F32),
                        pltpu.VMEM((8 * ts, 256), F32)])
    return pl.pallas_call(
        kern,
        out_shape=[jax.ShapeDtypeStruct((nb, ts, 256), F32),
                   jax.ShapeDtypeStruct((nb, ts, 256), F32)],
        grid_spec=grid_spec,
        compiler_params=_cp(("parallel", "arbitrary")),
        name="paged_attn",
    )(page_table, *([cache_mla] * g), *([cache_dk] * g), *([cache_dv] * g),
      qlat, qr, dq, row, dkn, dvn, wuv, lq1, lk1, lq2, lk2, gsub4)


def _outproj_kernel(x_ref, ao_ref, bo_ref, co_ref, w_ref, g_ref, o_ref):
    y = (_mm(ao_ref[...], w_ref[0:512, :]) + _mm(bo_ref[...], w_ref[512:768, :])
         + _mm(co_ref[...], w_ref[768:1024, :]))
    o_ref[...] = x_ref[...] + _rms(y, g_ref[...])


def _outproj(x, ao, bo, co, w, g):
    m, d = x.shape
    tm = _tile(m, 512)
    row = lambda n: pl.BlockSpec((tm, n), lambda i: (i, 0))
    full = lambda a: pl.BlockSpec(a.shape, lambda i: (0, 0))
    return pl.pallas_call(
        _outproj_kernel,
        out_shape=jax.ShapeDtypeStruct((m, d), F32),
        grid=(m // tm,),
        in_specs=[row(d), row(512), row(256), row(256), full(w), full(g)],
        out_specs=row(d),
        compiler_params=_cp(("parallel",)),
        name="out_proj",
    )(x, ao, bo, co, w, g)


def _pack_w_in(w):
    d = w.shape[0]
    a_qkv, a_z, a_b, a_a, b_cq, b_ckv, b_kr, c_q, c_k, c_v = jnp.split(
        w, np.cumsum([1536, 512, 4, 4, 192, 128, 32, 256, 256]).tolist(), axis=1)
    z = lambda n: jnp.zeros((d, n), w.dtype)
    kr_sw = jnp.concatenate([-b_kr[:, 16:32], b_kr[:, 0:16]], axis=1)
    v4 = c_v.reshape(d, 4, 64)
    vdup = jnp.concatenate([v4, v4], axis=2).reshape(d, 512)
    return jnp.concatenate(
        [a_qkv, a_z, a_b, a_a, z(120), b_cq, z(64), b_ckv, jnp.tile(b_kr, (1, 4)),
         jnp.tile(kr_sw, (1, 4)), c_q, c_k, vdup], axis=1).astype(BF16)


def _pack_mla(w_uq, w_uk, w_uv):
    nope = w_uq[:, :, 0:64].reshape(192, 256)
    rope = w_uq[:, :, 64:96]
    rope_sw = jnp.concatenate([-rope[:, :, 16:32], rope[:, :, 0:16]], axis=2)
    wuq = jnp.concatenate([nope, rope.reshape(192, 128), rope_sw.reshape(192, 128)], axis=1)
    wuq = jnp.concatenate([wuq, jnp.zeros((64, 512), wuq.dtype)], axis=0)
    eye = jnp.eye(4, dtype=w_uk.dtype)
    wuk_bd = jnp.einsum('rhn,hg->hngr', w_uk, eye).reshape(256, 512)
    wuv_bd = jnp.einsum('rhe,hg->hrge', w_uv, eye).reshape(512, 256)
    return wuq.astype(BF16), wuk_bd.astype(BF16), wuv_bd.astype(BF16)


def _rope_tables(pos):
    freqs = jnp.power(ROPE_THETA, -jnp.arange(16, dtype=F32) / 16)
    ang = pos[:, None] * freqs[None, :]
    return jnp.tile(jnp.cos(ang), (1, 8)), jnp.tile(jnp.sin(ang), (1, 8))


def _lane_pad(v, n=LANES, offset=0):
    v = v.reshape(1, -1).astype(F32)
    return jnp.pad(v, ((0, 0), (offset, n - offset - v.shape[1])))


def kernel(x_prompt, x_sample, cache_mla, cache_diff_k, cache_diff_v, state_dn_S, state_dn_conv, page_table, g_f1_pre, g_f1_post, w_f1_gate, w_f1_up, w_f1_down, g_mix_pre, g_mix_post, w_in, w_out, dn_conv_w, dn_A_log, dn_dt_bias, dn_g_out, mla_g_q, mla_w_uq, mla_g_kv, mla_w_uk, mla_w_uv, diff_lam_q1, diff_lam_k1, diff_lam_q2, diff_lam_k2, diff_g_subln, g_f2_pre, g_f2_post, w_f2_gate, w_f2_up, w_f2_down):
    bp, tp, d = x_prompt.shape
    bs, ts, _ = x_sample.shape
    depth = w_in.shape[0]
    n_pages = page_table.shape[1]
    past_len = n_pages * PAGE
    mla_scale = 96.0 ** -0.5
    diff_scale = 32.0 ** -0.5

    cos_p, sin_p = _rope_tables(jnp.arange(tp, dtype=F32))
    cos_p, sin_p = jnp.tile(cos_p, (bp, 1)), jnp.tile(sin_p, (bp, 1))
    cos_s, sin_s = _rope_tables(past_len + jnp.arange(ts, dtype=F32))
    cos_s, sin_s = jnp.tile(cos_s, (bs, 1)), jnp.tile(sin_s, (bs, 1))

    n_pool = cache_mla.shape[1]
    cache_dk = cache_diff_k.reshape(depth, n_pool, PAGE, 256)
    cache_dv = cache_diff_v.reshape(depth, n_pool, PAGE, 256)
    zero_conv = jnp.zeros((1, bp, CONV_W - 1, 1536), F32)
    zero_s = jnp.zeros((1, bp, 4, 128, 128), F32)

    xp = x_prompt.reshape(bp * tp, d)
    xs = x_sample.reshape(bs * ts, d)
    p_out = [[] for _ in range(5)]
    s_out = [[] for _ in range(5)]
    row1 = lambda v: v.reshape(1, -1).astype(F32)
    for l in range(depth):
        lam_init = 0.8 - 0.6 * math.exp(-0.3 * l)
        wg1, wu1, wd1 = (w_f1_gate[l].astype(BF16), w_f1_up[l].astype(BF16),
                         w_f1_down[l].astype(BF16))
        wg2, wu2, wd2 = (w_f2_gate[l].astype(BF16), w_f2_up[l].astype(BF16),
                         w_f2_down[l].astype(BF16))
        w_inp = _pack_w_in(w_in[l])
        wuq, wuk_bd, wuv_bd = _pack_mla(mla_w_uq[l], mla_w_uk[l], mla_w_uv[l])
        w_o = w_out[l].astype(BF16)
        gq = _lane_pad(mla_g_q[l], 256)
        gkv = row1(mla_g_kv[l])
        alog = _lane_pad(dn_A_log[l], offset=4)
        dtb = _lane_pad(dn_dt_bias[l], offset=4)
        gout = row1(dn_g_out[l])
        cw = dn_conv_w[l].astype(F32)
        lq1, lk1, lq2, lk2 = (_lane_pad(diff_lam_q1[l]), _lane_pad(diff_lam_k1[l]),
                              _lane_pad(diff_lam_q2[l]), _lane_pad(diff_lam_k2[l]))
        gsub2 = jnp.tile(row1(diff_g_subln[l]), (1, 2))
        gsub4 = jnp.tile(row1(diff_g_subln[l]), (1, 4))

        def front(x, cos, sin):
            x = _ffn(x, row1(g_f1_pre[l]), wg1, wu1, wd1, row1(g_f1_post[l]))
            return x, _proj(x, row1(g_mix_pre[l]), w_inp, cos, sin, gq, wuq, wuk_bd, gkv)

        def back(x, ao, bo, co):
            x = _outproj(x, ao, bo, co, w_o, row1(g_mix_post[l]))
            return _ffn(x, row1(g_f2_pre[l]), wg2, wu2, wd2, row1(g_f2_post[l]))

        xp, (qkv, z, ba, mrow, kp, qlat, qrope, dq, dk, dv, dkb, dvb) = front(xp, cos_p, sin_p)
        ao, s_new = _gdn(qkv, z, ba, zero_conv, zero_s, 0, cw, alog, dtb, gout, nb=bp, t=tp)
        bo = _mla_attn(qlat, qrope, kp, wuv_bd, nb=bp, t=tp, scale=mla_scale)
        co = _diff_attn(dq, dkb, dvb, lq1, lk1, lq2, lk2, gsub2, nb=bp, t=tp,
                        scale=diff_scale, lam_init=lam_init)
        xp = back(xp, ao, bo, co)
        p_out[0].append(qkv.reshape(bp, tp, 1536)[:, tp - (CONV_W - 1):])
        p_out[1].append(s_new)
        p_out[2].append(mrow.reshape(bp, tp, 160))
        p_out[3].append(dk.reshape(bp, tp, 8, 32))
        p_out[4].append(dv.reshape(bp, tp, 4, 64))

        xs, (qkv, z, ba, mrow, kp, qlat, qrope, dq, dk, dv, dkb, dvb) = front(xs, cos_s, sin_s)
        ao, s_new = _gdn(qkv, z, ba, state_dn_conv, state_dn_S, l, cw, alog, dtb, gout,
                         nb=bs, t=ts)
        qlat_h = qlat.reshape(bs, ts, 4, 128).transpose(0, 2, 1, 3).reshape(bs, 4 * ts, 128)
        qr_h = qrope.reshape(bs, ts, 4, 32).transpose(0, 2, 1, 3).reshape(bs, 4 * ts, 32)
        bo, co = _paged(page_table, cache_mla, cache_dk, cache_dv, l, qlat_h, qr_h,
                        dq.reshape(bs, ts, 256).astype(F32), mrow.reshape(bs, ts, 160),
                        dk.reshape(bs, ts, 256), dv.reshape(bs, ts, 256), wuv_bd,
                        lq1, lk1, lq2, lk2, gsub4, nb=bs, ts=ts, mla_scale=mla_scale,
                        diff_scale=diff_scale, lam_init=lam_init)
        xs = back(xs, ao, bo.reshape(bs * ts, 256), co.reshape(bs * ts, 256))
        s_out[0].append(qkv.reshape(bs, ts, 1536)[:, ts - (CONV_W - 1):])
        s_out[1].append(s_new)
        s_out[2].append(mrow.reshape(bs, ts, 160))
        s_out[3].append(dk.reshape(bs, ts, 8, 32))
        s_out[4].append(dv.reshape(bs, ts, 4, 64))

    p_conv, p_s, p_mla, p_dk, p_dv = [jnp.stack(a, axis=0) for a in p_out]
    s_conv, s_s, s_mla, s_dk, s_dv = [jnp.stack(a, axis=0) for a in s_out]
    return (xp.reshape(bp, tp, d), xs.reshape(bs, ts, d), p_mla, p_dk, p_dv, p_s, p_conv,
            s_mla, s_dk, s_dv, s_s, s_conv)
```

```python
import functools
import math

import jax
import jax.numpy as jnp
import numpy as np
from jax import lax
from jax.experimental import pallas as pl
from jax.experimental.pallas import tpu as pltpu

F32 = jnp.float32
BF16 = jnp.bfloat16
EPS = 1e-6
ROPE_THETA = 10000.0
CONV_W = 4
DN_CHUNK = 64
PAGE = 128
LANES = 128
LOG2E = math.log2(math.e)
VMEM_LIMIT = 52 * 1024 * 1024
NEG = -1e30

_P_QKV, _P_Z, _P_BA, _P_CQ, _P_CKV, _P_KR, _P_KRS, _P_DQ, _P_DK, _P_DV, _P_END = (
    0, 1536, 2048, 2176, 2432, 2560, 2688, 2816, 3072, 3328, 3840)


def _cp(sem):
    return pltpu.CompilerParams(dimension_semantics=sem, vmem_limit_bytes=VMEM_LIMIT)


def _tile(m, pref):
    t = pref
    while m % t:
        t //= 2
    return t


def _mm(a, b):
    return jnp.dot(a.astype(BF16), b.astype(BF16), preferred_element_type=F32)


def _mm_nt(a, b):
    return lax.dot_general(a.astype(BF16), b.astype(BF16), (((1,), (1,)), ((), ())),
                           preferred_element_type=F32)


def _mm_tn(a, b):
    return lax.dot_general(a.astype(BF16), b.astype(BF16), (((0,), (0,)), ((), ())),
                           preferred_element_type=F32)


def _rms(x, g, n=None):
    n = x.shape[-1] if n is None else n
    ms = jnp.sum(x * x, axis=-1, keepdims=True) * (1.0 / n)
    return x * lax.rsqrt(ms + EPS) * g


def _silu(x):
    return x / (1.0 + jnp.exp(-x))


def _lane_group(lane, width, idx):
    return (lane >= idx * width) & (lane < (idx + 1) * width)


def _rep(x, k):
    return x if k == 1 else jnp.concatenate([x] * k, axis=1)


def _ffn_kernel(x_ref, gpre_ref, wg_ref, wu_ref, wd_ref, gpost_ref, o_ref, h_scr, acc_scr):
    f = pl.program_id(1)

    @pl.when(f == 0)
    def _():
        h_scr[...] = _rms(x_ref[...], gpre_ref[...]).astype(BF16)

    h = h_scr[...]
    a = jnp.dot(h, wg_ref[...], preferred_element_type=F32)
    b = jnp.dot(h, wu_ref[...], preferred_element_type=F32)
    act = (_silu(a) * b).astype(BF16)
    part = jnp.dot(act, wd_ref[...], preferred_element_type=F32)

    @pl.when(f == 0)
    def _():
        acc_scr[...] = part

    @pl.when(f != 0)
    def _():
        acc_scr[...] += part

    @pl.when(f == pl.num_programs(1) - 1)
    def _():
        o_ref[...] = x_ref[...] + 0.5 * _rms(acc_scr[...], gpost_ref[...])


def _ffn(x, gpre, wg, wu, wd, gpost):
    m, d = x.shape
    ff = wg.shape[1]
    tm = _tile(m, 512)
    tf = ff // 2 if (ff // 2) % LANES == 0 else ff
    return pl.pallas_call(
        _ffn_kernel,
        out_shape=jax.ShapeDtypeStruct((m, d), F32),
        grid=(m // tm, ff // tf),
        in_specs=[
            pl.BlockSpec((tm, d), lambda i, f: (i, 0)),
            pl.BlockSpec((1, d), lambda i, f: (0, 0)),
            pl.BlockSpec((d, tf), lambda i, f: (0, f)),
            pl.BlockSpec((d, tf), lambda i, f: (0, f)),
            pl.BlockSpec((tf, d), lambda i, f: (f, 0)),
            pl.BlockSpec((1, d), lambda i, f: (0, 0)),
        ],
        out_specs=pl.BlockSpec((tm, d), lambda i, f: (i, 0)),
        scratch_shapes=[pltpu.VMEM((tm, d), BF16), pltpu.VMEM((tm, d), F32)],
        compiler_params=_cp(("parallel", "arbitrary")),
        name="ffn",
    )(x, gpre, wg, wu, wd, gpost)


def _proj_kernel(x_ref, g_ref, w_ref, cos_ref, sin_ref, gq_ref, wuq_ref, wuk_ref, gkv_ref,
                 qkv_ref, z_ref, ba_ref, row_ref, kp_ref, qlat_ref, qrope_ref,
                 dq_ref, dk_ref, dv_ref, dkb_ref, dvb_ref, *, mla_qscale, diff_qscale):
    h = _rms(x_ref[...], g_ref[...]).astype(BF16)

    def proj(lo, hi):
        return jnp.dot(h, w_ref[:, lo:hi], preferred_element_type=F32)

    qkv_ref[...] = proj(_P_QKV, _P_Z)
    z_ref[...] = proj(_P_Z, _P_BA)
    ba_ref[...] = proj(_P_BA, _P_CQ)
    cos = cos_ref[...]
    sin = sin_ref[...]
    cq = _rms(proj(_P_CQ, _P_CKV), gq_ref[...], n=192)
    qb = _mm(cq, wuq_ref[...])
    qlat_ref[...] = (_mm(qb[:, 0:256], wuk_ref[...]) * mla_qscale).astype(BF16)
    qrope_ref[...] = ((qb[:, 256:384] * cos + qb[:, 384:512] * sin) * mla_qscale).astype(BF16)
    ckv = _rms(proj(_P_CKV, _P_KR), gkv_ref[...])
    kr = proj(_P_KR, _P_KRS) * cos + proj(_P_KRS, _P_DQ) * sin
    row_ref[:, 0:128] = ckv
    row_ref[:, 128:160] = kr[:, 0:32]
    kp_ref[:, 0:128] = ckv.astype(BF16)
    kp_ref[:, 128:256] = kr.astype(BF16)
    dq_ref[...] = (proj(_P_DQ, _P_DK) * diff_qscale).astype(BF16)
    dk = proj(_P_DK, _P_DV)
    dk_ref[...] = dk
    dkb_ref[...] = dk.astype(BF16)
    dvd = proj(_P_DV, _P_END)
    dvb_ref[...] = dvd.astype(BF16)
    lane = lax.broadcasted_iota(jnp.int32, (dvd.shape[0], LANES), 1)
    for j in range(2):
        dv_ref[:, j * 128:(j + 1) * 128] = jnp.where(
            lane < 64, dvd[:, (2 * j) * 128:(2 * j + 1) * 128],
            dvd[:, (2 * j + 1) * 128:(2 * j + 2) * 128])


def _proj(x, g, w, cos, sin, gq, wuq, wuk, gkv, *, mla_qscale, diff_qscale):
    m, d = x.shape
    tm = _tile(m, 512)
    row = lambda n: pl.BlockSpec((tm, n), lambda i: (i, 0))
    full = lambda a: pl.BlockSpec(a.shape, lambda i: (0, 0))
    outs = [(1536, F32), (512, F32), (128, F32), (160, F32), (256, BF16), (512, BF16),
            (128, BF16), (256, BF16), (256, F32), (256, F32), (256, BF16), (512, BF16)]
    kern = functools.partial(_proj_kernel, mla_qscale=mla_qscale, diff_qscale=diff_qscale)
    return pl.pallas_call(
        kern,
        out_shape=[jax.ShapeDtypeStruct((m, n), dt) for n, dt in outs],
        grid=(m // tm,),
        in_specs=[row(d), full(g), full(w), row(128), row(128), full(gq), full(wuq),
                  full(wuk), full(gkv)],
        out_specs=[row(n) for n, _ in outs],
        compiler_params=_cp(("parallel",)),
        name="proj_in",
    )(x, g, w, cos, sin, gq, wuq, wuk, gkv)


def _gdn_kernel(x_ref, z_ref, ba_ref, conv0_ref, s0_ref, cw_ref, alog_ref, dtb_ref, gout_ref,
                o_ref, sout_ref, xs_scr, s_scr, *, nbb, t_in, t_cmp, t_valid):
    C = DN_CHUNK
    nc = t_cmp // C
    tb = pl.program_id(1)

    @pl.when(tb == 0)
    def _():
        s_scr[...] = s0_ref[...]
        for bb in range(nbb):
            xs_scr[bb, 5:8, :] = conv0_ref[bb]

    @pl.when(tb != 0)
    def _():
        for bb in range(nbb):
            xs_scr[bb, 5:8, :] = xs_scr[bb, 5 + t_in:8 + t_in, :]

    for bb in range(nbb):
        xs_scr[bb, 8:8 + t_in, :] = x_ref[bb * t_in:(bb + 1) * t_in, :]
        if t_cmp > t_in:
            xs_scr[bb, 8 + t_in:8 + t_cmp, :] = jnp.zeros((t_cmp - t_in, xs_scr.shape[2]), F32)

    ri = lax.broadcasted_iota(jnp.int32, (C, C), 0)
    ci = lax.broadcasted_iota(jnp.int32, (C, C), 1)
    incl = ri >= ci
    strict = ri > ci
    tril = jnp.where(incl, 1.0, 0.0).astype(F32)
    alog = alog_ref[...]
    dtb = dtb_ref[...]
    gout = gout_ref[...]

    trip = []
    for bb in range(nbb):
        for c in range(nc):
            r0 = c * C
            y = xs_scr[bb, 5 + r0:5 + r0 + C, :] * cw_ref[0:1, :]
            for i in range(1, CONV_W):
                y = y + xs_scr[bb, 5 + r0 + i:5 + r0 + i + C, :] * cw_ref[i:i + 1, :]
            qkv = _silu(y)
            if t_in >= t_cmp:
                ba = ba_ref[bb * t_in + r0:bb * t_in + r0 + C, :]
            else:
                ba = jnp.concatenate([ba_ref[bb * t_in:(bb + 1) * t_in, :],
                                      jnp.zeros((t_cmp - t_in, LANES), F32)], axis=0)
            bet_all = 1.0 / (1.0 + jnp.exp(-ba))
            xa = ba + dtb
            g_all = -jnp.exp(alog) * (jnp.maximum(xa, 0.0)
                                      + jnp.log(1.0 + jnp.exp(-jnp.abs(xa))))
            if t_valid < t_cmp:
                valid = lax.broadcasted_iota(jnp.int32, (C, 1), 0) + r0 < t_valid
                qkv = jnp.where(valid, qkv, 0.0)
                bet_all = jnp.where(valid, bet_all, 0.0)
                g_all = jnp.where(valid, g_all, 0.0)
            gam_all = jnp.dot(tril, g_all, preferred_element_type=F32,
                              precision=lax.Precision.HIGHEST)
            gam_t = gam_all.T
            for h in range(4):
                q = qkv[:, h * 128:(h + 1) * 128]
                k = qkv[:, 512 + h * 128:512 + (h + 1) * 128]
                v = qkv[:, 1024 + h * 128:1024 + (h + 1) * 128]
                q = q * lax.rsqrt(jnp.sum(q * q, axis=-1, keepdims=True) + EPS) * (128.0 ** -0.5)
                k = k * lax.rsqrt(jnp.sum(k * k, axis=-1, keepdims=True) + EPS)
                bet = bet_all[:, h:h + 1]
                gcol = gam_all[:, 4 + h:5 + h]
                grow = gam_t[4 + h:5 + h, :]
                glast = gam_all[C - 1:C, 4 + h:5 + h]
                diff = gcol - grow
                decay = jnp.where(incl, jnp.exp(jnp.where(incl, diff, 0.0)), 0.0)
                eg = jnp.exp(gcol)
                trip.append(dict(bb=bb, c=c, h=h, q=q, k=k, v=v, bet=bet, decay=decay, eg=eg,
                                 kd=k * jnp.exp(glast - gcol), sdec=jnp.exp(glast)))

    for t in trip:
        t["kk"] = _mm_nt(t["k"], t["k"])
    for t in trip:
        t["qk"] = _mm_nt(t["q"], t["k"]) * t["decay"]
    for t in trip:
        t["tp"] = jnp.where(strict, t["bet"] * t["kk"] * t["decay"], 0.0)
        t["x"] = jnp.concatenate([t["v"] * t["bet"], t["k"] * (t["bet"] * t["eg"])], axis=-1)
    for t in trip:
        t["x"] = t["x"] - _mm(t["tp"], t["x"])
    for _ in range(5):
        for t in trip:
            t["tp"] = _mm(t["tp"], t["tp"])
        for t in trip:
            t["x"] = t["x"] + _mm(t["tp"], t["x"])
    state = {(bb, h): s_scr[bb, h] for bb in range(nbb) for h in range(4)}
    for c in range(nc):
        cur = [t for t in trip if t["c"] == c]
        for t in cur:
            s_h = state[(t["bb"], t["h"])]
            t["ws"] = _mm(t["x"][:, 128:256], s_h)
            t["qs"] = _mm(t["q"], s_h)
        for t in cur:
            t["u"] = t["x"][:, 0:128] - t["ws"]
        for t in cur:
            t["o"] = t["eg"] * t["qs"] + _mm(t["qk"], t["u"])
            key = (t["bb"], t["h"])
            state[key] = t["sdec"] * state[key] + _mm_tn(t["kd"], t["u"])
        r0 = c * C
        if r0 < t_in:
            n = min(C, t_in - r0)
            for t in cur:
                bb, h = t["bb"], t["h"]
                zz = z_ref[bb * t_in + r0:bb * t_in + r0 + n, h * 128:(h + 1) * 128]
                o_ref[bb * t_in + r0:bb * t_in + r0 + n, h * 128:(h + 1) * 128] = (
                    _rms(t["o"][0:n, :], gout) * _silu(zz)).astype(o_ref.dtype)
    for (bb, h), s_h in state.items():
        s_scr[bb, h] = s_h

    @pl.when(tb == pl.num_programs(1) - 1)
    def _():
        sout_ref[...] = s_scr[...]


def _gdn(qkv, z, ba, conv0, s0, layer, cw, alog, dtb, gout, *, nb, t):
    if t >= 128:
        nbb = 1
        t_in = t_cmp = t_valid = _tile(t, 256)
    else:
        nbb = _tile(nb, 4)
        t_in, t_cmp, t_valid = t, DN_CHUNK, t
    nt = t // t_in
    rows = lambda n: pl.BlockSpec((nbb * t_in, n), lambda b, j: (b * nt + j, 0))
    full = lambda a: pl.BlockSpec(a.shape, lambda b, j: (0,) * a.ndim)
    kern = functools.partial(_gdn_kernel, nbb=nbb, t_in=t_in, t_cmp=t_cmp, t_valid=t_valid)
    return pl.pallas_call(
        kern,
        out_shape=[jax.ShapeDtypeStruct((nb * t, 512), BF16 if t_in % 16 == 0 else F32),
                   jax.ShapeDtypeStruct((nb, 4, 128, 128), F32)],
        grid=(nb // nbb, nt),
        in_specs=[rows(1536), rows(512), rows(128),
                  pl.BlockSpec((None, nbb, 3, 1536), lambda b, j: (layer, b, 0, 0)),
                  pl.BlockSpec((None, nbb, 4, 128, 128), lambda b, j: (layer, b, 0, 0, 0)),
                  full(cw), full(alog), full(dtb), full(gout)],
        out_specs=[rows(512), pl.BlockSpec((nbb, 4, 128, 128), lambda b, j: (b, 0, 0, 0))],
        scratch_shapes=[pltpu.VMEM((nbb, 8 + t_cmp, 1536), F32),
                        pltpu.VMEM((nbb, 4, 128, 128), F32)],
        compiler_params=_cp(("parallel", "arbitrary")),
        name="gdn",
    )(qkv, z, ba, conv0, s0, cw, alog, dtb, gout)


def _causal_sweep(i, tq, tk, nheads, step):
    n_full = (i * tq) // tk
    n_tot = ((i + 1) * tq + tk - 1) // tk

    def full_body(j, carry):
        step(j, None)
        return carry

    lax.fori_loop(0, n_full, full_body, 0)

    def diag_body(j, carry):
        qpos = i * tq + lax.broadcasted_iota(jnp.int32, (tq, tk), 0)
        kpos = j * tk + lax.broadcasted_iota(jnp.int32, (tq, tk), 1)
        mask = kpos <= qpos
        step(j, jnp.concatenate([mask] * nheads, axis=0))
        return carry

    lax.fori_loop(n_full, n_tot, diag_body, 0)


def _online_softmax(s, m_scr, tk):
    m_prev = m_scr[...]
    m_next = jnp.maximum(m_prev, jnp.max(s, axis=-1, keepdims=True))
    p = jnp.exp2(s - _rep(m_next, tk // LANES)).astype(BF16)
    alpha = jnp.exp2(m_prev - m_next)
    m_scr[...] = m_next
    return p, alpha


def _mla_attn_kernel(qlat_ref, qrope_ref, kp_ref, wuv_ref, o_ref, q_scr, m_scr, acc_scr,
                     *, tq, tk):
    i = pl.program_id(1)
    qr = qrope_ref[...]
    lane = lax.broadcasted_iota(jnp.int32, qr.shape, 1)
    for h in range(4):
        q_scr[h * tq:(h + 1) * tq, 0:128] = qlat_ref[:, h * 128:(h + 1) * 128]
        q_scr[h * tq:(h + 1) * tq, 128:256] = jnp.where(_lane_group(lane, 32, h), qr,
                                                        jnp.zeros_like(qr))
    m_scr[...] = jnp.full(m_scr.shape, -jnp.inf, F32)
    acc_scr[...] = jnp.zeros(acc_scr.shape, F32)
    ones = jnp.ones((tk, LANES), BF16)

    def step(j, mask):
        kc = kp_ref[pl.ds(pl.multiple_of(j * tk, tk), tk), :]
        s = lax.dot_general(q_scr[...], kc, (((1,), (1,)), ((), ())),
                            preferred_element_type=F32)
        if mask is not None:
            s = jnp.where(mask, s, NEG)
        p, alpha = _online_softmax(s, m_scr, tk)
        vext = jnp.concatenate([kc[:, 0:128], ones], axis=1)
        acc_scr[...] = acc_scr[...] * _rep(alpha, 2) + jnp.dot(
            p, vext, preferred_element_type=F32)

    _causal_sweep(i, tq, tk, 4, step)
    o = (acc_scr[:, 0:128] / acc_scr[:, 128:256]).astype(BF16)
    bo = jnp.dot(o[0:tq], wuv_ref[0:128, :], preferred_element_type=F32)
    for h in range(1, 4):
        bo = bo + jnp.dot(o[h * tq:(h + 1) * tq], wuv_ref[h * 128:(h + 1) * 128, :],
                          preferred_element_type=F32)
    o_ref[...] = bo.astype(BF16)


def _mla_attn(qlat, qrope, kp, wuv, *, nb, t):
    tq = _tile(t, 256)
    tk = _tile(t, 512)
    nq = t // tq
    kern = functools.partial(_mla_attn_kernel, tq=tq, tk=tk)
    return pl.pallas_call(
        kern,
        out_shape=jax.ShapeDtypeStruct((nb * t, 256), BF16),
        grid=(nb, nq),
        in_specs=[pl.BlockSpec((tq, 512), lambda b, i: (b * nq + i, 0)),
                  pl.BlockSpec((tq, 128), lambda b, i: (b * nq + i, 0)),
                  pl.BlockSpec((t, 256), lambda b, i: (b, 0)),
                  pl.BlockSpec(wuv.shape, lambda b, i: (0, 0))],
        out_specs=pl.BlockSpec((tq, 256), lambda b, i: (b * nq + i, 0)),
        scratch_shapes=[pltpu.VMEM((4 * tq, 256), BF16), pltpu.VMEM((4 * tq, LANES), F32),
                        pltpu.VMEM((4 * tq, 256), F32)],
        compiler_params=_cp(("parallel", "parallel")),
        name="mla_attn",
    )(qlat, qrope, kp, wuv)


def _diff_lambda(lq1, lk1, lq2, lk2, lam_init):
    return (jnp.exp(jnp.sum(lq1 * lk1, axis=-1, keepdims=True))
            - jnp.exp(jnp.sum(lq2 * lk2, axis=-1, keepdims=True)) + lam_init)


def _diff_attn_kernel(q_ref, k_ref, v_ref, lq1_ref, lk1_ref, lq2_ref, lk2_ref, gsub_ref, o_ref,
                      q_scr, m_scr, acc_scr, *, tq, tk, lam_init):
    i = pl.program_id(1)
    q = q_ref[...]
    lane = lax.broadcasted_iota(jnp.int32, q.shape, 1)
    for h in range(8):
        q_scr[h * tq:(h + 1) * tq, :] = jnp.where(_lane_group(lane, 32, h), q, jnp.zeros_like(q))
    m_scr[...] = jnp.full(m_scr.shape, -jnp.inf, F32)
    acc_scr[...] = jnp.zeros(acc_scr.shape, F32)
    ones = jnp.ones((tk, LANES), BF16)

    def step(j, mask):
        start = pl.multiple_of(j * tk, tk)
        kc = k_ref[pl.ds(start, tk), :]
        s = lax.dot_general(q_scr[...], kc, (((1,), (1,)), ((), ())),
                            preferred_element_type=F32)
        if mask is not None:
            s = jnp.where(mask, s, NEG)
        p, alpha = _online_softmax(s, m_scr, tk)
        alpha = _rep(alpha, 2)
        for c in range(4):
            vext = jnp.concatenate([v_ref[pl.ds(start, tk), c * 128:(c + 1) * 128], ones], axis=1)
            r = slice(2 * c * tq, (2 * c + 2) * tq)
            acc_scr[r, :] = acc_scr[r, :] * alpha[r, :] + jnp.dot(
                p[r, :], vext, preferred_element_type=F32)

    _causal_sweep(i, tq, tk, 8, step)
    lam = _diff_lambda(lq1_ref[...], lk1_ref[...], lq2_ref[...], lk2_ref[...], lam_init)
    on = acc_scr[:, 0:128] / acc_scr[:, 128:256]
    lane128 = lax.broadcasted_iota(jnp.int32, (tq, LANES), 1)
    outs = []
    for c in range(4):
        d = on[2 * c * tq:(2 * c + 1) * tq] - lam * on[(2 * c + 1) * tq:(2 * c + 2) * tq]
        outs.append(_rms(d, gsub_ref[...]) * (1.0 - lam_init))
    for j in range(2):
        o_ref[:, j * 128:(j + 1) * 128] = jnp.where(
            lane128 < 64, outs[2 * j], outs[2 * j + 1]).astype(BF16)


def _diff_attn(q, k, vdup, lq1, lk1, lq2, lk2, gsub, *, nb, t, lam_init):
    tq = _tile(t, 128)
    tk = _tile(t, 512)
    nq = t // tq
    kern = functools.partial(_diff_attn_kernel, tq=tq, tk=tk, lam_init=lam_init)
    vec = lambda a: pl.BlockSpec(a.shape, lambda b, i: (0, 0))
    return pl.pallas_call(
        kern,
        out_shape=jax.ShapeDtypeStruct((nb * t, 256), BF16),
        grid=(nb, nq),
        in_specs=[pl.BlockSpec((tq, 256), lambda b, i: (b * nq + i, 0)),
                  pl.BlockSpec((t, 256), lambda b, i: (b, 0)),
                  pl.BlockSpec((t, 512), lambda b, i: (b, 0)),
                  vec(lq1), vec(lk1), vec(lq2), vec(lk2), vec(gsub)],
        out_specs=pl.BlockSpec((tq, 256), lambda b, i: (b * nq + i, 0)),
        scratch_shapes=[pltpu.VMEM((8 * tq, 256), BF16), pltpu.VMEM((8 * tq, LANES), F32),
                        pltpu.VMEM((8 * tq, 256), F32)],
        compiler_params=_cp(("parallel", "parallel")),
        name="diff_attn",
    )(q, k, vdup, lq1, lk1, lq2, lk2, gsub)


def _paged_kernel(pt_ref, *refs, g, ts, lam_init):
    mla_pages = refs[0:g]
    dk_pages = refs[g:2 * g]
    dv_pages = refs[2 * g:3 * g]
    (qlat_ref, qr_ref, dq_ref, row_ref, dkn_ref, dvn_ref, wuv_ref, lq1_ref, lk1_ref, lq2_ref,
     lk2_ref, gsub_ref, bo_ref, co_ref, dqb_scr, ma_scr, la_scr, aa_scr, md_scr, ld_scr,
     ad_scr) = refs[3 * g:]
    del pt_ref
    j = pl.program_id(1)

    @pl.when(j == 0)
    def _():
        q = dq_ref[...]
        lane = lax.broadcasted_iota(jnp.int32, q.shape, 1)
        for h in range(8):
            dqb_scr[h * ts:(h + 1) * ts, :] = jnp.where(_lane_group(lane, 32, h), q, 0.0)
        ma_scr[...] = jnp.full(ma_scr.shape, -jnp.inf, F32)
        la_scr[...] = jnp.zeros(la_scr.shape, F32)
        aa_scr[...] = jnp.zeros(aa_scr.shape, F32)
        md_scr[...] = jnp.full(md_scr.shape, -jnp.inf, F32)
        ld_scr[...] = jnp.zeros(ld_scr.shape, F32)
        ad_scr[...] = jnp.zeros(ad_scr.shape, F32)

    qlat = qlat_ref[...].astype(BF16)
    qr = qr_ref[...].astype(BF16)
    dqb = dqb_scr[...].astype(BF16)

    def update(s_list, pv, m_scr, l_scr, acc_scr):
        s = jnp.concatenate(s_list, axis=-1) if len(s_list) > 1 else s_list[0]
        m_old = m_scr[...]
        m_new = jnp.maximum(m_old, jnp.max(s, axis=-1, keepdims=True))
        p = jnp.exp2(s - m_new)
        corr = jnp.exp2(m_old - m_new)
        l_scr[...] = l_scr[...] * corr + jnp.sum(p, axis=-1, keepdims=True)
        pb = p.astype(BF16)
        acc = acc_scr[...] * corr
        off = 0
        for idx, sb in enumerate(s_list):
            n = sb.shape[1]
            acc = acc + pv(pb[:, off:off + n], idx)
            off += n
        acc_scr[...] = acc
        m_scr[...] = m_new

    kts = [r[...].astype(BF16) for r in mla_pages]
    sa = [jnp.dot(qlat, kt[0:128, :], preferred_element_type=F32)
          + jnp.dot(qr, kt[128:160, :], preferred_element_type=F32) for kt in kts]
    update(sa, lambda pb, idx: _mm_nt(pb, kts[idx][0:128, :]), ma_scr, la_scr, aa_scr)
    sd = [jnp.dot(dqb, r[...].astype(BF16), preferred_element_type=F32) for r in dk_pages]
    update(sd, lambda pb, idx: _mm_nt(pb, dv_pages[idx][...]), md_scr, ld_scr, ad_scr)

    @pl.when(j == pl.num_programs(1) - 1)
    def _():
        tsp = -(-ts // 16) * 16
        qpos = lax.broadcasted_iota(jnp.int32, (ts, tsp), 0)
        kpos = lax.broadcasted_iota(jnp.int32, (ts, tsp), 1)
        mask = kpos <= qpos

        def padded(ref):
            rows = ref[...]
            return jnp.concatenate([rows, jnp.zeros((tsp - ts, rows.shape[1]), F32)],
                                   axis=0).astype(BF16)

        rows = padded(row_ref)
        s = _mm_nt(qlat, rows[:, 0:128]) + _mm_nt(qr, rows[:, 128:160])
        update([jnp.where(jnp.concatenate([mask] * 4, axis=0), s, NEG)],
               lambda pb, idx: jnp.dot(pb, rows[:, 0:128], preferred_element_type=F32),
               ma_scr, la_scr, aa_scr)
        s = _mm_nt(dqb, padded(dkn_ref))
        vn = padded(dvn_ref)
        update([jnp.where(jnp.concatenate([mask] * 8, axis=0), s, NEG)],
               lambda pb, idx: jnp.dot(pb, vn, preferred_element_type=F32),
               md_scr, ld_scr, ad_scr)
        o = (aa_scr[...] / la_scr[...]).astype(BF16)
        bo = jnp.zeros((ts, 256), F32)
        for h in range(4):
            full = jnp.dot(o, wuv_ref[h * 128:(h + 1) * 128, :], preferred_element_type=F32)
            bo = bo + full[h * ts:(h + 1) * ts]
        bo_ref[...] = bo
        on = ad_scr[...] / ld_scr[...]
        lam = _diff_lambda(lq1_ref[...], lk1_ref[...], lq2_ref[...], lk2_ref[...], lam_init)
        lane = lax.broadcasted_iota(jnp.int32, (ts, 256), 1)
        d = jnp.zeros((ts, 256), F32)
        for c in range(4):
            dc = on[2 * c * ts:(2 * c + 1) * ts] - lam * on[(2 * c + 1) * ts:(2 * c + 2) * ts]
            d = jnp.where(_lane_group(lane, 64, c), dc, d)
        ms = jnp.zeros((ts, 256), F32)
        for c in range(4):
            sel = _lane_group(lane, 64, c)
            ssq = jnp.sum(jnp.where(sel, d * d, 0.0), axis=-1, keepdims=True) * (1.0 / 64)
            ms = jnp.where(sel, ssq, ms)
        co_ref[...] = d * lax.rsqrt(ms + EPS) * gsub_ref[...] * (1.0 - lam_init)


def _paged(page_table, cache_mla_t, cache_dk_t, cache_dv_t, layer, qlat, qr, dq, row, dkn, dvn,
           wuv, lq1, lk1, lq2, lk2, gsub4, *, nb, ts, lam_init):
    n_pages = page_table.shape[1]
    g = _tile(n_pages, 8)
    kern = functools.partial(_paged_kernel, g=g, ts=ts, lam_init=lam_init)

    def page_spec(feat, gi):
        return pl.BlockSpec((None, None, feat, PAGE),
                            lambda b, j, pt: (layer, pt[b, j * g + gi], 0, 0))

    per_b = lambda r, n: pl.BlockSpec((None, r, n), lambda b, j, pt: (b, 0, 0))
    vec = lambda a: pl.BlockSpec(a.shape, lambda b, j, pt: (0, 0))
    in_specs = ([page_spec(160, gi) for gi in range(g)] + [page_spec(256, gi) for gi in range(g)]
                + [page_spec(256, gi) for gi in range(g)]
                + [per_b(4 * ts, 128), per_b(4 * ts, 32), per_b(ts, 256), per_b(ts, 160),
                   per_b(ts, 256), per_b(ts, 256), vec(wuv), vec(lq1), vec(lk1), vec(lq2),
                   vec(lk2), vec(gsub4)])
    grid_spec = pltpu.PrefetchScalarGridSpec(
        num_scalar_prefetch=1,
        grid=(nb, n_pages // g),
        in_specs=in_specs,
        out_specs=[per_b(ts, 256), per_b(ts, 256)],
        scratch_shapes=[pltpu.VMEM((8 * ts, 256), F32),
                        pltpu.VMEM((4 * ts, 1), F32), pltpu.VMEM((4 * ts, 1), F32),
                        pltpu.VMEM((4 * ts, 128), F32),
                        pltpu.VMEM((8 * ts, 1), F32), pltpu.VMEM((8 * ts, 1), F32),
                        pltpu.VMEM((8 * ts, 256), F32)])
    return pl.pallas_call(
        kern,
        out_shape=[jax.ShapeDtypeStruct((nb, ts, 256), F32),
                   jax.ShapeDtypeStruct((nb, ts, 256), F32)],
        grid_spec=grid_spec,
        compiler_params=_cp(("parallel", "arbitrary")),
        name="paged_attn",
    )(page_table, *([cache_mla_t] * g), *([cache_dk_t] * g), *([cache_dv_t] * g),
      qlat, qr, dq, row, dkn, dvn, wuv, lq1, lk1, lq2, lk2, gsub4)


def _outproj_kernel(x_ref, ao_ref, bo_ref, co_ref, w_ref, g_ref, o_ref):
    y = (_mm(ao_ref[...], w_ref[0:512, :]) + _mm(bo_ref[...], w_ref[512:768, :])
         + _mm(co_ref[...], w_ref[768:1024, :]))
    o_ref[...] = x_ref[...] + _rms(y, g_ref[...])


def _outproj(x, ao, bo, co, w, g):
    m, d = x.shape
    tm = _tile(m, 512)
    row = lambda n: pl.BlockSpec((tm, n), lambda i: (i, 0))
    full = lambda a: pl.BlockSpec(a.shape, lambda i: (0, 0))
    return pl.pallas_call(
        _outproj_kernel,
        out_shape=jax.ShapeDtypeStruct((m, d), F32),
        grid=(m // tm,),
        in_specs=[row(d), row(512), row(256), row(256), full(w), full(g)],
        out_specs=row(d),
        compiler_params=_cp(("parallel",)),
        name="out_proj",
    )(x, ao, bo, co, w, g)


def _pack_w_in(w):
    d = w.shape[0]
    a_qkv, a_z, a_b, a_a, b_cq, b_ckv, b_kr, c_q, c_k, c_v = jnp.split(
        w, np.cumsum([1536, 512, 4, 4, 192, 128, 32, 256, 256]).tolist(), axis=1)
    z = lambda n: jnp.zeros((d, n), w.dtype)
    kr_sw = jnp.concatenate([-b_kr[:, 16:32], b_kr[:, 0:16]], axis=1)
    v4 = c_v.reshape(d, 4, 64)
    vdup = jnp.concatenate([v4, v4], axis=2).reshape(d, 512)
    return jnp.concatenate(
        [a_qkv, a_z, a_b, a_a, z(120), b_cq, z(64), b_ckv, jnp.tile(b_kr, (1, 4)),
         jnp.tile(kr_sw, (1, 4)), c_q, c_k, vdup], axis=1).astype(BF16)


def _pack_mla(w_uq, w_uk, w_uv):
    nope = w_uq[:, :, 0:64].reshape(192, 256)
    rope = w_uq[:, :, 64:96]
    rope_sw = jnp.concatenate([-rope[:, :, 16:32], rope[:, :, 0:16]], axis=2)
    wuq = jnp.concatenate([nope, rope.reshape(192, 128), rope_sw.reshape(192, 128)], axis=1)
    wuq = jnp.concatenate([wuq, jnp.zeros((64, 512), wuq.dtype)], axis=0)
    eye = jnp.eye(4, dtype=w_uk.dtype)
    wuk_bd = jnp.einsum('rhn,hg->hngr', w_uk, eye).reshape(256, 512)
    wuv_bd = jnp.einsum('rhe,hg->hrge', w_uv, eye).reshape(512, 256)
    return wuq.astype(BF16), wuk_bd.astype(BF16), wuv_bd.astype(BF16)


def _rope_tables(pos):
    freqs = jnp.power(ROPE_THETA, -jnp.arange(16, dtype=F32) / 16)
    ang = pos[:, None] * freqs[None, :]
    return jnp.tile(jnp.cos(ang), (1, 8)), jnp.tile(jnp.sin(ang), (1, 8))


def _lane_pad(v, n=LANES, offset=0):
    v = v.reshape(1, -1).astype(F32)
    return jnp.pad(v, ((0, 0), (offset, n - offset - v.shape[1])))


def kernel(x_prompt, x_sample, cache_mla, cache_diff_k, cache_diff_v, state_dn_S, state_dn_conv, page_table, g_f1_pre, g_f1_post, w_f1_gate, w_f1_up, w_f1_down, g_mix_pre, g_mix_post, w_in, w_out, dn_conv_w, dn_A_log, dn_dt_bias, dn_g_out, mla_g_q, mla_w_uq, mla_g_kv, mla_w_uk, mla_w_uv, diff_lam_q1, diff_lam_k1, diff_lam_q2, diff_lam_k2, diff_g_subln, g_f2_pre, g_f2_post, w_f2_gate, w_f2_up, w_f2_down):
    bp, tp, d = x_prompt.shape
    bs, ts, _ = x_sample.shape
    depth = w_in.shape[0]
    n_pages = page_table.shape[1]
    past_len = n_pages * PAGE
    mla_qscale = 96.0 ** -0.5 * LOG2E
    diff_qscale = 32.0 ** -0.5 * LOG2E

    cos_p, sin_p = _rope_tables(jnp.arange(tp, dtype=F32))
    cos_p, sin_p = jnp.tile(cos_p, (bp, 1)), jnp.tile(sin_p, (bp, 1))
    cos_s, sin_s = _rope_tables(past_len + jnp.arange(ts, dtype=F32))
    cos_s, sin_s = jnp.tile(cos_s, (bs, 1)), jnp.tile(sin_s, (bs, 1))

    n_pool = cache_mla.shape[1]
    cache_mla_t = jnp.swapaxes(cache_mla, 2, 3)
    cache_dk_t = jnp.transpose(cache_diff_k, (0, 1, 3, 4, 2)).reshape(depth, n_pool, 256, PAGE)
    cache_dv_t = jnp.transpose(cache_diff_v, (0, 1, 3, 4, 2)).reshape(depth, n_pool, 256, PAGE)
    zero_conv = jnp.zeros((1, bp, CONV_W - 1, 1536), F32)
    zero_s = jnp.zeros((1, bp, 4, 128, 128), F32)

    xp = x_prompt.reshape(bp * tp, d)
    xs = x_sample.reshape(bs * ts, d)
    p_out = [[] for _ in range(5)]
    s_out = [[] for _ in range(5)]
    row1 = lambda v: v.reshape(1, -1).astype(F32)
    for l in range(depth):
        lam_init = 0.8 - 0.6 * math.exp(-0.3 * l)
        wg1, wu1, wd1 = (w_f1_gate[l].astype(BF16), w_f1_up[l].astype(BF16),
                         w_f1_down[l].astype(BF16))
        wg2, wu2, wd2 = (w_f2_gate[l].astype(BF16), w_f2_up[l].astype(BF16),
                         w_f2_down[l].astype(BF16))
        w_inp = _pack_w_in(w_in[l])
        wuq, wuk_bd, wuv_bd = _pack_mla(mla_w_uq[l], mla_w_uk[l], mla_w_uv[l])
        w_o = w_out[l].astype(BF16)
        gq = _lane_pad(mla_g_q[l], 256)
        gkv = row1(mla_g_kv[l])
        alog = _lane_pad(dn_A_log[l], offset=4)
        dtb = _lane_pad(dn_dt_bias[l], offset=4)
        gout = row1(dn_g_out[l])
        cw = dn_conv_w[l].astype(F32)
        lq1, lk1, lq2, lk2 = (_lane_pad(diff_lam_q1[l]), _lane_pad(diff_lam_k1[l]),
                              _lane_pad(diff_lam_q2[l]), _lane_pad(diff_lam_k2[l]))
        gsub2 = jnp.tile(row1(diff_g_subln[l]), (1, 2))
        gsub4 = jnp.tile(row1(diff_g_subln[l]), (1, 4))

        def front(x, cos, sin):
            x = _ffn(x, row1(g_f1_pre[l]), wg1, wu1, wd1, row1(g_f1_post[l]))
            return x, _proj(x, row1(g_mix_pre[l]), w_inp, cos, sin, gq, wuq, wuk_bd, gkv,
                            mla_qscale=mla_qscale, diff_qscale=diff_qscale)

        def back(x, ao, bo, co):
            x = _outproj(x, ao, bo, co, w_o, row1(g_mix_post[l]))
            return _ffn(x, row1(g_f2_pre[l]), wg2, wu2, wd2, row1(g_f2_post[l]))

        xp, (qkv, z, ba, mrow, kp, qlat, qrope, dq, dk, dv, dkb, dvb) = front(xp, cos_p, sin_p)
        ao, s_new = _gdn(qkv, z, ba, zero_conv, zero_s, 0, cw, alog, dtb, gout, nb=bp, t=tp)
        bo = _mla_attn(qlat, qrope, kp, wuv_bd, nb=bp, t=tp)
        co = _diff_attn(dq, dkb, dvb, lq1, lk1, lq2, lk2, gsub2, nb=bp, t=tp, lam_init=lam_init)
        xp = back(xp, ao, bo, co)
        p_out[0].append(qkv.reshape(bp, tp, 1536)[:, tp - (CONV_W - 1):])
        p_out[1].append(s_new)
        p_out[2].append(mrow.reshape(bp, tp, 160))
        p_out[3].append(dk.reshape(bp, tp, 8, 32))
        p_out[4].append(dv.reshape(bp, tp, 4, 64))

        xs, (qkv, z, ba, mrow, kp, qlat, qrope, dq, dk, dv, dkb, dvb) = front(xs, cos_s, sin_s)
        ao, s_new = _gdn(qkv, z, ba, state_dn_conv, state_dn_S, l, cw, alog, dtb, gout,
                         nb=bs, t=ts)
        qlat_h = qlat.reshape(bs, ts, 4, 128).transpose(0, 2, 1, 3).reshape(bs, 4 * ts, 128)
        qr_h = qrope.reshape(bs, ts, 4, 32).transpose(0, 2, 1, 3).reshape(bs, 4 * ts, 32)
        bo, co = _paged(page_table, cache_mla_t, cache_dk_t, cache_dv_t, l, qlat_h, qr_h,
                        dq.reshape(bs, ts, 256).astype(F32), mrow.reshape(bs, ts, 160),
                        dk.reshape(bs, ts, 256), dv.reshape(bs, ts, 256), wuv_bd,
                        lq1, lk1, lq2, lk2, gsub4, nb=bs, ts=ts, lam_init=lam_init)
        xs = back(xs, ao, bo.reshape(bs * ts, 256), co.reshape(bs * ts, 256))
        s_out[0].append(qkv.reshape(bs, ts, 1536)[:, ts - (CONV_W - 1):])
        s_out[1].append(s_new)
        s_out[2].append(mrow.reshape(bs, ts, 160))
        s_out[3].append(dk.reshape(bs, ts, 8, 32))
        s_out[4].append(dv.reshape(bs, ts, 4, 64))

    p_conv, p_s, p_mla, p_dk, p_dv = [jnp.stack(a, axis=0) for a in p_out]
    s_conv, s_s, s_mla, s_dk, s_dv = [jnp.stack(a, axis=0) for a in s_out]
    return (xp.reshape(bp, tp, d), xs.reshape(bs, ts, d), p_mla, p_dk, p_dv, p_s, p_conv,
            s_mla, s_dk, s_dv, s_s, s_conv)
```

```python
import functools
import math

import jax
import jax.numpy as jnp
import numpy as np
from jax import lax
from jax.experimental import pallas as pl
from jax.experimental.pallas import tpu as pltpu

F32 = jnp.float32
BF16 = jnp.bfloat16
EPS = 1e-6
ROPE_THETA = 10000.0
CONV_W = 4
DN_CHUNK = 64
PAGE = 128
LANES = 128
LOG2E = math.log2(math.e)
MXU_N = 256
VMEM_LIMIT = 52 * 1024 * 1024
NEG = -1e30

_P_QKV, _P_Z, _P_BA, _P_CQ, _P_KR, _P_DQ, _P_DK, _P_DV, _P_END = (
    0, 1536, 2048, 2304, 2560, 2816, 3072, 3328, 3840)


def _cp(sem):
    return pltpu.CompilerParams(dimension_semantics=sem, vmem_limit_bytes=VMEM_LIMIT)


def _tile(m, pref):
    t = pref
    while m % t:
        t //= 2
    return t


def _mm(a, b):
    return jnp.dot(a.astype(BF16), b.astype(BF16), preferred_element_type=F32)


def _mm_nt(a, b):
    return lax.dot_general(a.astype(BF16), b.astype(BF16), (((1,), (1,)), ((), ())),
                           preferred_element_type=F32)


def _mm_tn(a, b):
    return lax.dot_general(a.astype(BF16), b.astype(BF16), (((0,), (0,)), ((), ())),
                           preferred_element_type=F32)


def _rms(x, g, n=None):
    n = x.shape[-1] if n is None else n
    ms = jnp.sum(x * x, axis=-1, keepdims=True) * (1.0 / n)
    return x * lax.rsqrt(ms + EPS) * g


def _silu(x):
    return x / (1.0 + jnp.exp(-x))


def _lane_group(lane, width, idx):
    return (lane >= idx * width) & (lane < (idx + 1) * width)


def _rep(x, k):
    return x if k == 1 else jnp.concatenate([x] * k, axis=1)


def _ffn_body(x, gpre_ref, wg_ref, wu_ref, wd_ref, gpost_ref, act_scr):
    h = _rms(x, gpre_ref[...]).astype(BF16)
    ff = wg_ref.shape[1]
    cw = MXU_N
    for c in range(ff // cw):
        a = jnp.dot(h, wg_ref[:, c * cw:(c + 1) * cw], preferred_element_type=F32)
        b = jnp.dot(h, wu_ref[:, c * cw:(c + 1) * cw], preferred_element_type=F32)
        act_scr[:, c * cw:(c + 1) * cw] = (_silu(a) * b).astype(BF16)
    y = jnp.dot(act_scr[...], wd_ref[...], preferred_element_type=F32)
    return x + 0.5 * _rms(y, gpost_ref[...])


def _ffn_kernel(x_ref, gpre_ref, wg_ref, wu_ref, wd_ref, gpost_ref, o_ref, act_scr):
    o_ref[...] = _ffn_body(x_ref[...], gpre_ref, wg_ref, wu_ref, wd_ref, gpost_ref, act_scr)


def _outproj_ffn_kernel(x_ref, ao_ref, bo_ref, co_ref, wo_ref, gmix_ref, gpre_ref, wg_ref, wu_ref,
                        wd_ref, gpost_ref, o_ref, act_scr):
    y = (_mm(ao_ref[...], wo_ref[0:512, :]) + _mm(bo_ref[...], wo_ref[512:768, :])
         + _mm(co_ref[...], wo_ref[768:1024, :]))
    x = x_ref[...] + _rms(y, gmix_ref[...])
    o_ref[...] = _ffn_body(x, gpre_ref, wg_ref, wu_ref, wd_ref, gpost_ref, act_scr)


def _resident(a):
    return pl.BlockSpec(a.shape, lambda i: (0,) * a.ndim, pipeline_mode=pl.Buffered(1))


def _ffn(x, gpre, wg, wu, wd, gpost, mix=None):
    m, d = x.shape
    ff = wg.shape[1]
    assert ff % MXU_N == 0
    tm = _tile(m, 512)
    row = lambda n: pl.BlockSpec((tm, n), lambda i: (i, 0))
    weights = [gpre, wg, wu, wd, gpost]
    if mix is None:
        kern, ins, in_specs = _ffn_kernel, [x], [row(d)]
    else:
        ao, bo, co, wo, gmix = mix
        kern, ins = _outproj_ffn_kernel, [x, ao, bo, co, wo, gmix]
        in_specs = [row(d), row(512), row(256), row(256), _resident(wo), _resident(gmix)]
    return pl.pallas_call(
        kern,
        out_shape=jax.ShapeDtypeStruct((m, d), F32),
        grid=(m // tm,),
        in_specs=in_specs + [_resident(w) for w in weights],
        out_specs=row(d),
        scratch_shapes=[pltpu.VMEM((tm, ff), BF16)],
        compiler_params=_cp(("parallel",)),
        name="ffn" if mix is None else "outproj_ffn",
    )(*ins, *weights)


def _proj_kernel(x_ref, g_ref, w_ref, cos_ref, sin_ref, gq_ref, wuq_ref, wuk_ref,
                 gkv_ref, qkv_ref, z_ref, ba_ref, row_ref, kp_ref, qlat_ref, qrope_ref,
                 dq_ref, dk_ref, dv_ref, dkb_ref, dvb_ref, *, mla_qscale, diff_qscale):
    h = _rms(x_ref[...], g_ref[...]).astype(BF16)

    def proj(lo, hi):
        return jnp.dot(h, w_ref[:, lo:hi], preferred_element_type=F32)

    qkv_ref[...] = proj(_P_QKV, _P_Z)
    z_ref[...] = proj(_P_Z, _P_BA)
    ba_ckv = proj(_P_BA, _P_CQ)
    ba_ref[...] = ba_ckv[:, 0:128]
    cos = cos_ref[...]
    sin = sin_ref[...]
    cq = _rms(proj(_P_CQ, _P_KR), gq_ref[...], n=192)
    qb = _mm(cq, wuq_ref[...])
    qlat_ref[...] = (_mm(qb[:, 0:256], wuk_ref[...]) * mla_qscale).astype(BF16)
    qrope_ref[...] = ((qb[:, 256:384] * cos + qb[:, 384:512] * sin) * mla_qscale).astype(BF16)
    ckv = _rms(ba_ckv[:, 128:256], gkv_ref[...])
    kr2 = proj(_P_KR, _P_DQ)
    kr = kr2[:, 0:128] * cos + kr2[:, 128:256] * sin
    row_ref[:, 0:128] = ckv
    row_ref[:, 128:160] = kr[:, 0:32]
    kp_ref[:, 0:128] = ckv.astype(BF16)
    kp_ref[:, 128:256] = kr.astype(BF16)
    dq_ref[...] = (proj(_P_DQ, _P_DK) * diff_qscale).astype(BF16)
    dk = proj(_P_DK, _P_DV)
    dk_ref[...] = dk
    dkb_ref[...] = dk.astype(BF16)
    dvd = proj(_P_DV, _P_END)
    dvb_ref[...] = dvd.astype(BF16)
    lane = lax.broadcasted_iota(jnp.int32, (dvd.shape[0], LANES), 1)
    for j in range(2):
        dv_ref[:, j * 128:(j + 1) * 128] = jnp.where(
            lane < 64, dvd[:, (2 * j) * 128:(2 * j + 1) * 128],
            dvd[:, (2 * j + 1) * 128:(2 * j + 2) * 128])


def _proj(x, g, w, cos, sin, gq, wuq, wuk, gkv, *, mla_qscale, diff_qscale):
    m, d = x.shape
    tm = _tile(m, 512)
    row = lambda n: pl.BlockSpec((tm, n), lambda i: (i, 0))
    full = _resident
    outs = [(1536, F32), (512, F32), (128, F32), (160, F32), (256, BF16), (512, BF16),
            (128, BF16), (256, BF16), (256, F32), (256, F32), (256, BF16), (512, BF16)]
    kern = functools.partial(_proj_kernel, mla_qscale=mla_qscale, diff_qscale=diff_qscale)
    return pl.pallas_call(
        kern,
        out_shape=[jax.ShapeDtypeStruct((m, n), dt) for n, dt in outs],
        grid=(m // tm,),
        in_specs=[row(d), full(g), full(w), row(128), row(128), full(gq), full(wuq),
                  full(wuk), full(gkv)],
        out_specs=[row(n) for n, _ in outs],
        compiler_params=_cp(("parallel",)),
        name="proj_in",
    )(x, g, w, cos, sin, gq, wuq, wuk, gkv)


def _gdn_kernel(x_ref, z_ref, ba_ref, conv0_ref, s0_ref, cw_ref, alog_ref, dtb_ref, gout_ref,
                o_ref, sout_ref, xs_scr, s_scr, *, nbb, t_in, t_cmp, t_valid):
    C = DN_CHUNK
    nc = t_cmp // C
    tb = pl.program_id(1)

    @pl.when(tb == 0)
    def _():
        s_scr[...] = s0_ref[...]
        for bb in range(nbb):
            xs_scr[bb, 5:8, :] = conv0_ref[bb]

    @pl.when(tb != 0)
    def _():
        for bb in range(nbb):
            xs_scr[bb, 5:8, :] = xs_scr[bb, 5 + t_in:8 + t_in, :]

    for bb in range(nbb):
        xs_scr[bb, 8:8 + t_in, :] = x_ref[bb * t_in:(bb + 1) * t_in, :]
        if t_cmp > t_in:
            xs_scr[bb, 8 + t_in:8 + t_cmp, :] = jnp.zeros((t_cmp - t_in, xs_scr.shape[2]), F32)

    ri = lax.broadcasted_iota(jnp.int32, (C, C), 0)
    ci = lax.broadcasted_iota(jnp.int32, (C, C), 1)
    incl = ri >= ci
    strict = ri > ci
    tril = jnp.where(incl, 1.0, 0.0).astype(F32)
    alog = alog_ref[...]
    dtb = dtb_ref[...]
    gout = gout_ref[...]

    trip = []
    for bb in range(nbb):
        for c in range(nc):
            r0 = c * C
            y = xs_scr[bb, 5 + r0:5 + r0 + C, :] * cw_ref[0:1, :]
            for i in range(1, CONV_W):
                y = y + xs_scr[bb, 5 + r0 + i:5 + r0 + i + C, :] * cw_ref[i:i + 1, :]
            qkv = _silu(y)
            if t_in >= t_cmp:
                ba = ba_ref[bb * t_in + r0:bb * t_in + r0 + C, :]
            else:
                ba = jnp.concatenate([ba_ref[bb * t_in:(bb + 1) * t_in, :],
                                      jnp.zeros((t_cmp - t_in, LANES), F32)], axis=0)
            bet_all = 1.0 / (1.0 + jnp.exp(-ba))
            xa = ba + dtb
            g_all = -jnp.exp(alog) * (jnp.maximum(xa, 0.0)
                                      + jnp.log(1.0 + jnp.exp(-jnp.abs(xa))))
            if t_valid < t_cmp:
                valid = lax.broadcasted_iota(jnp.int32, (C, 1), 0) + r0 < t_valid
                qkv = jnp.where(valid, qkv, 0.0)
                bet_all = jnp.where(valid, bet_all, 0.0)
                g_all = jnp.where(valid, g_all, 0.0)
            gam_all = jnp.dot(tril, g_all, preferred_element_type=F32,
                              precision=lax.Precision.HIGHEST)
            gam_t = gam_all.T
            for h in range(4):
                q = qkv[:, h * 128:(h + 1) * 128]
                k = qkv[:, 512 + h * 128:512 + (h + 1) * 128]
                v = qkv[:, 1024 + h * 128:1024 + (h + 1) * 128]
                q = q * lax.rsqrt(jnp.sum(q * q, axis=-1, keepdims=True) + EPS) * (128.0 ** -0.5)
                k = k * lax.rsqrt(jnp.sum(k * k, axis=-1, keepdims=True) + EPS)
                bet = bet_all[:, h:h + 1]
                gcol = gam_all[:, 4 + h:5 + h]
                grow = gam_t[4 + h:5 + h, :]
                glast = gam_all[C - 1:C, 4 + h:5 + h]
                diff = gcol - grow
                decay = jnp.where(incl, jnp.exp(jnp.where(incl, diff, 0.0)), 0.0)
                eg = jnp.exp(gcol)
                trip.append(dict(bb=bb, c=c, h=h, q=q, k=k, v=v, bet=bet, decay=decay, eg=eg,
                                 kd=k * jnp.exp(glast - gcol), sdec=jnp.exp(glast)))

    for t in trip:
        t["kk"] = _mm_nt(t["k"], t["k"])
    for t in trip:
        t["qk"] = _mm_nt(t["q"], t["k"]) * t["decay"]
    for t in trip:
        t["tp"] = jnp.where(strict, t["bet"] * t["kk"] * t["decay"], 0.0)
        t["x"] = jnp.concatenate([t["v"] * t["bet"], t["k"] * (t["bet"] * t["eg"])], axis=-1)
    b16r, b16c = lax.shift_right_logical(ri, 4), lax.shift_right_logical(ci, 4)
    b32r, b32c = lax.shift_right_logical(ri, 5), lax.shift_right_logical(ci, 5)
    in16 = b16r == b16c
    in32 = b32r == b32c
    eye = jnp.where(ri == ci, 1.0, 0.0).astype(F32)
    for t in trip:
        t["td"] = jnp.where(in16, t["tp"], 0.0)
    for t in trip:
        t["s2"] = _mm(t["td"], t["td"])
    for t in trip:
        t["s4"] = _mm(t["s2"], t["s2"])
        t["a"] = eye - t["td"] + t["s2"] - _mm(t["td"], t["s2"])
    for t in trip:
        t["s8"] = _mm(t["s4"], t["s4"])
    for t in trip:
        t["bm"] = t["s4"] + t["s8"] + _mm(t["s4"], t["s8"])
    for t in trip:
        t["p"] = t["a"] + _mm(t["a"], t["bm"])
    for t in trip:
        t["lp"] = _mm(jnp.where(in32 & ~in16, t["tp"], 0.0), t["p"])
    for t in trip:
        t["p"] = t["p"] - _mm(t["p"], t["lp"])
    for t in trip:
        t["lp"] = _mm(jnp.where(in32, 0.0, t["tp"]), t["p"])
    for t in trip:
        t["p"] = t["p"] - _mm(t["p"], t["lp"])
    for t in trip:
        t["x"] = _mm(t["p"], t["x"])
    state = {(bb, h): s_scr[bb, h] for bb in range(nbb) for h in range(4)}
    for c in range(nc):
        cur = [t for t in trip if t["c"] == c]
        for t in cur:
            s_h = state[(t["bb"], t["h"])]
            t["ws"] = _mm(t["x"][:, 128:256], s_h)
            t["qs"] = _mm(t["q"], s_h)
        for t in cur:
            t["u"] = t["x"][:, 0:128] - t["ws"]
        for t in cur:
            t["o"] = t["eg"] * t["qs"] + _mm(t["qk"], t["u"])
            key = (t["bb"], t["h"])
            state[key] = t["sdec"] * state[key] + _mm_tn(t["kd"], t["u"])
        r0 = c * C
        if r0 < t_in:
            n = min(C, t_in - r0)
            for t in cur:
                bb, h = t["bb"], t["h"]
                zz = z_ref[bb * t_in + r0:bb * t_in + r0 + n, h * 128:(h + 1) * 128]
                o_ref[bb * t_in + r0:bb * t_in + r0 + n, h * 128:(h + 1) * 128] = (
                    _rms(t["o"][0:n, :], gout) * _silu(zz)).astype(o_ref.dtype)
    for (bb, h), s_h in state.items():
        s_scr[bb, h] = s_h

    @pl.when(tb == pl.num_programs(1) - 1)
    def _():
        sout_ref[...] = s_scr[...]


def _gdn(qkv, z, ba, conv0, s0, layer, cw, alog, dtb, gout, *, nb, t):
    if t >= 128:
        nbb = 1
        t_in = t_cmp = t_valid = _tile(t, 256)
    else:
        nbb = _tile(nb, 4)
        t_in, t_cmp, t_valid = t, DN_CHUNK, t
    nt = t // t_in
    rows = lambda n: pl.BlockSpec((nbb * t_in, n), lambda b, j: (b * nt + j, 0))
    full = lambda a: pl.BlockSpec(a.shape, lambda b, j: (0,) * a.ndim)
    kern = functools.partial(_gdn_kernel, nbb=nbb, t_in=t_in, t_cmp=t_cmp, t_valid=t_valid)
    return pl.pallas_call(
        kern,
        out_shape=[jax.ShapeDtypeStruct((nb * t, 512), BF16 if t_in % 16 == 0 else F32),
                   jax.ShapeDtypeStruct((nb, 4, 128, 128), F32)],
        grid=(nb // nbb, nt),
        in_specs=[rows(1536), rows(512), rows(128),
                  pl.BlockSpec((None, nbb, 3, 1536), lambda b, j: (layer, b, 0, 0)),
                  pl.BlockSpec((None, nbb, 4, 128, 128), lambda b, j: (layer, b, 0, 0, 0)),
                  full(cw), full(alog), full(dtb), full(gout)],
        out_specs=[rows(512), pl.BlockSpec((nbb, 4, 128, 128), lambda b, j: (b, 0, 0, 0))],
        scratch_shapes=[pltpu.VMEM((nbb, 8 + t_cmp, 1536), F32),
                        pltpu.VMEM((nbb, 4, 128, 128), F32)],
        compiler_params=_cp(("parallel", "arbitrary")),
        name="gdn",
    )(qkv, z, ba, conv0, s0, cw, alog, dtb, gout)


def _causal_sweep(i, tq, tk, nheads, step):
    n_full = (i * tq) // tk
    n_tot = ((i + 1) * tq + tk - 1) // tk

    def full_body(j, carry):
        step(j, None)
        return carry

    lax.fori_loop(0, n_full, full_body, 0)

    def diag_body(j, carry):
        qpos = i * tq + lax.broadcasted_iota(jnp.int32, (tq, tk), 0)
        kpos = j * tk + lax.broadcasted_iota(jnp.int32, (tq, tk), 1)
        mask = kpos <= qpos
        step(j, jnp.concatenate([mask] * nheads, axis=0))
        return carry

    lax.fori_loop(n_full, n_tot, diag_body, 0)


def _online_softmax(s, m_scr, tk):
    m_prev = m_scr[...]
    m_next = jnp.maximum(m_prev, jnp.max(s, axis=-1, keepdims=True))
    p = jnp.exp2(s - _rep(m_next, tk // LANES)).astype(BF16)
    alpha = jnp.exp2(m_prev - m_next)
    m_scr[...] = m_next
    return p, alpha


def _mla_attn_kernel(qlat_ref, qrope_ref, kp_ref, wuv_ref, o_ref, q_scr, m_scr, acc_scr,
                     *, tq, tk):
    i = pl.program_id(1)
    qr = qrope_ref[...]
    lane = lax.broadcasted_iota(jnp.int32, qr.shape, 1)
    for h in range(4):
        q_scr[h * tq:(h + 1) * tq, 0:128] = qlat_ref[:, h * 128:(h + 1) * 128]
        q_scr[h * tq:(h + 1) * tq, 128:256] = jnp.where(_lane_group(lane, 32, h), qr,
                                                        jnp.zeros_like(qr))
    m_scr[...] = jnp.full(m_scr.shape, -jnp.inf, F32)
    acc_scr[...] = jnp.zeros(acc_scr.shape, F32)
    ones = jnp.ones((tk, LANES), BF16)

    def step(j, mask):
        kc = kp_ref[pl.ds(pl.multiple_of(j * tk, tk), tk), :]
        s = lax.dot_general(q_scr[...], kc, (((1,), (1,)), ((), ())),
                            preferred_element_type=F32)
        if mask is not None:
            s = jnp.where(mask, s, NEG)
        p, alpha = _online_softmax(s, m_scr, tk)
        vext = jnp.concatenate([kc[:, 0:128], ones], axis=1)
        acc_scr[...] = acc_scr[...] * _rep(alpha, 2) + jnp.dot(
            p, vext, preferred_element_type=F32)

    _causal_sweep(i, tq, tk, 4, step)
    o = (acc_scr[:, 0:128] / acc_scr[:, 128:256]).astype(BF16)
    bo = jnp.dot(o[0:tq], wuv_ref[0:128, :], preferred_element_type=F32)
    for h in range(1, 4):
        bo = bo + jnp.dot(o[h * tq:(h + 1) * tq], wuv_ref[h * 128:(h + 1) * 128, :],
                          preferred_element_type=F32)
    o_ref[...] = bo.astype(BF16)


def _mla_attn(qlat, qrope, kp, wuv, *, nb, t):
    tq = _tile(t, 256)
    tk = _tile(t, 512)
    nq = t // tq
    kern = functools.partial(_mla_attn_kernel, tq=tq, tk=tk)
    return pl.pallas_call(
        kern,
        out_shape=jax.ShapeDtypeStruct((nb * t, 256), BF16),
        grid=(nb, nq),
        in_specs=[pl.BlockSpec((tq, 512), lambda b, i: (b * nq + i, 0)),
                  pl.BlockSpec((tq, 128), lambda b, i: (b * nq + i, 0)),
                  pl.BlockSpec((t, 256), lambda b, i: (b, 0)),
                  pl.BlockSpec(wuv.shape, lambda b, i: (0, 0))],
        out_specs=pl.BlockSpec((tq, 256), lambda b, i: (b * nq + i, 0)),
        scratch_shapes=[pltpu.VMEM((4 * tq, 256), BF16), pltpu.VMEM((4 * tq, LANES), F32),
                        pltpu.VMEM((4 * tq, 256), F32)],
        compiler_params=_cp(("parallel", "parallel")),
        name="mla_attn",
    )(qlat, qrope, kp, wuv)


def _diff_lambda(lq1, lk1, lq2, lk2, lam_init):
    return (jnp.exp(jnp.sum(lq1 * lk1, axis=-1, keepdims=True))
            - jnp.exp(jnp.sum(lq2 * lk2, axis=-1, keepdims=True)) + lam_init)


def _diff_attn_kernel(q_ref, k_ref, v_ref, lq1_ref, lk1_ref, lq2_ref, lk2_ref, gsub_ref, o_ref,
                      q_scr, m_scr, acc_scr, *, tq, tk, lam_init):
    i = pl.program_id(1)
    q = q_ref[...]
    lane = lax.broadcasted_iota(jnp.int32, q.shape, 1)
    for h in range(8):
        q_scr[h * tq:(h + 1) * tq, :] = jnp.where(_lane_group(lane, 32, h), q, jnp.zeros_like(q))
    m_scr[...] = jnp.full(m_scr.shape, -jnp.inf, F32)
    acc_scr[...] = jnp.zeros(acc_scr.shape, F32)
    ones = jnp.ones((tk, LANES), BF16)

    def step(j, mask):
        start = pl.multiple_of(j * tk, tk)
        kc = k_ref[pl.ds(start, tk), :]
        s = lax.dot_general(q_scr[...], kc, (((1,), (1,)), ((), ())),
                            preferred_element_type=F32)
        if mask is not None:
            s = jnp.where(mask, s, NEG)
        p, alpha = _online_softmax(s, m_scr, tk)
        alpha = _rep(alpha, 2)
        for c in range(4):
            vext = jnp.concatenate([v_ref[pl.ds(start, tk), c * 128:(c + 1) * 128], ones], axis=1)
            r = slice(2 * c * tq, (2 * c + 2) * tq)
            acc_scr[r, :] = acc_scr[r, :] * alpha[r, :] + jnp.dot(
                p[r, :], vext, preferred_element_type=F32)

    _causal_sweep(i, tq, tk, 8, step)
    lam = _diff_lambda(lq1_ref[...], lk1_ref[...], lq2_ref[...], lk2_ref[...], lam_init)
    on = acc_scr[:, 0:128] / acc_scr[:, 128:256]
    lane128 = lax.broadcasted_iota(jnp.int32, (tq, LANES), 1)
    outs = []
    for c in range(4):
        d = on[2 * c * tq:(2 * c + 1) * tq] - lam * on[(2 * c + 1) * tq:(2 * c + 2) * tq]
        outs.append(_rms(d, gsub_ref[...]) * (1.0 - lam_init))
    for j in range(2):
        o_ref[:, j * 128:(j + 1) * 128] = jnp.where(
            lane128 < 64, outs[2 * j], outs[2 * j + 1]).astype(BF16)


def _diff_attn(q, k, vdup, lq1, lk1, lq2, lk2, gsub, *, nb, t, lam_init):
    tq = _tile(t, 128)
    tk = _tile(t, 512)
    nq = t // tq
    kern = functools.partial(_diff_attn_kernel, tq=tq, tk=tk, lam_init=lam_init)
    vec = lambda a: pl.BlockSpec(a.shape, lambda b, i: (0, 0))
    return pl.pallas_call(
        kern,
        out_shape=jax.ShapeDtypeStruct((nb * t, 256), BF16),
        grid=(nb, nq),
        in_specs=[pl.BlockSpec((tq, 256), lambda b, i: (b * nq + i, 0)),
                  pl.BlockSpec((t, 256), lambda b, i: (b, 0)),
                  pl.BlockSpec((t, 512), lambda b, i: (b, 0)),
                  vec(lq1), vec(lk1), vec(lq2), vec(lk2), vec(gsub)],
        out_specs=pl.BlockSpec((tq, 256), lambda b, i: (b * nq + i, 0)),
        scratch_shapes=[pltpu.VMEM((8 * tq, 256), BF16), pltpu.VMEM((8 * tq, LANES), F32),
                        pltpu.VMEM((8 * tq, 256), F32)],
        compiler_params=_cp(("parallel", "parallel")),
        name="diff_attn",
    )(q, k, vdup, lq1, lk1, lq2, lk2, gsub)


def _paged_kernel(pt_ref, *refs, g, ts, lam_init):
    mla_pages = refs[0:g]
    dk_pages = refs[g:2 * g]
    dv_pages = refs[2 * g:3 * g]
    (qlat_ref, qr_ref, dq_ref, row_ref, dkn_ref, dvn_ref, wuv_ref, lq1_ref, lk1_ref, lq2_ref,
     lk2_ref, gsub_ref, bo_ref, co_ref, dqb_scr, ma_scr, la_scr, aa_scr, md_scr, ld_scr,
     ad_scr) = refs[3 * g:]
    del pt_ref
    j = pl.program_id(1)

    @pl.when(j == 0)
    def _():
        q = dq_ref[...]
        lane = lax.broadcasted_iota(jnp.int32, q.shape, 1)
        for h in range(8):
            dqb_scr[h * ts:(h + 1) * ts, :] = jnp.where(_lane_group(lane, 32, h), q, 0.0)
        ma_scr[...] = jnp.full(ma_scr.shape, -jnp.inf, F32)
        la_scr[...] = jnp.zeros(la_scr.shape, F32)
        aa_scr[...] = jnp.zeros(aa_scr.shape, F32)
        md_scr[...] = jnp.full(md_scr.shape, -jnp.inf, F32)
        ld_scr[...] = jnp.zeros(ld_scr.shape, F32)
        ad_scr[...] = jnp.zeros(ad_scr.shape, F32)

    qlat = qlat_ref[...].astype(BF16)
    qr = qr_ref[...].astype(BF16)
    dqb = dqb_scr[...].astype(BF16)

    def update(s_list, pv, m_scr, l_scr, acc_scr):
        s = jnp.concatenate(s_list, axis=-1) if len(s_list) > 1 else s_list[0]
        m_old = m_scr[...]
        m_new = jnp.maximum(m_old, jnp.max(s, axis=-1, keepdims=True))
        p = jnp.exp2(s - m_new)
        corr = jnp.exp2(m_old - m_new)
        l_scr[...] = l_scr[...] * corr + jnp.sum(p, axis=-1, keepdims=True)
        pb = p.astype(BF16)
        acc = acc_scr[...] * corr
        off = 0
        for idx, sb in enumerate(s_list):
            n = sb.shape[1]
            acc = acc + pv(pb[:, off:off + n], idx)
            off += n
        acc_scr[...] = acc
        m_scr[...] = m_new

    kts = [r[...].astype(BF16) for r in mla_pages]
    sa = [jnp.dot(qlat, kt[0:128, :], preferred_element_type=F32)
          + jnp.dot(qr, kt[128:160, :], preferred_element_type=F32) for kt in kts]
    update(sa, lambda pb, idx: _mm_nt(pb, kts[idx][0:128, :]), ma_scr, la_scr, aa_scr)
    sd = [jnp.dot(dqb, r[...].astype(BF16), preferred_element_type=F32) for r in dk_pages]
    update(sd, lambda pb, idx: _mm_nt(pb, dv_pages[idx][...]), md_scr, ld_scr, ad_scr)

    @pl.when(j == pl.num_programs(1) - 1)
    def _():
        tsp = -(-ts // 16) * 16
        qpos = lax.broadcasted_iota(jnp.int32, (ts, tsp), 0)
        kpos = lax.broadcasted_iota(jnp.int32, (ts, tsp), 1)
        mask = kpos <= qpos

        def padded(ref):
            rows = ref[...]
            return jnp.concatenate([rows, jnp.zeros((tsp - ts, rows.shape[1]), F32)],
                                   axis=0).astype(BF16)

        rows = padded(row_ref)
        s = _mm_nt(qlat, rows[:, 0:128]) + _mm_nt(qr, rows[:, 128:160])
        update([jnp.where(jnp.concatenate([mask] * 4, axis=0), s, NEG)],
               lambda pb, idx: jnp.dot(pb, rows[:, 0:128], preferred_element_type=F32),
               ma_scr, la_scr, aa_scr)
        s = _mm_nt(dqb, padded(dkn_ref))
        vn = padded(dvn_ref)
        update([jnp.where(jnp.concatenate([mask] * 8, axis=0), s, NEG)],
               lambda pb, idx: jnp.dot(pb, vn, preferred_element_type=F32),
               md_scr, ld_scr, ad_scr)
        o = (aa_scr[...] / la_scr[...]).astype(BF16)
        bo = jnp.zeros((ts, 256), F32)
        for h in range(4):
            full = jnp.dot(o, wuv_ref[h * 128:(h + 1) * 128, :], preferred_element_type=F32)
            bo = bo + full[h * ts:(h + 1) * ts]
        bo_ref[...] = bo
        on = ad_scr[...] / ld_scr[...]
        lam = _diff_lambda(lq1_ref[...], lk1_ref[...], lq2_ref[...], lk2_ref[...], lam_init)
        lane = lax.broadcasted_iota(jnp.int32, (ts, 256), 1)
        d = jnp.zeros((ts, 256), F32)
        for c in range(4):
            dc = on[2 * c * ts:(2 * c + 1) * ts] - lam * on[(2 * c + 1) * ts:(2 * c + 2) * ts]
            d = jnp.where(_lane_group(lane, 64, c), dc, d)
        ms = jnp.zeros((ts, 256), F32)
        for c in range(4):
            sel = _lane_group(lane, 64, c)
            ssq = jnp.sum(jnp.where(sel, d * d, 0.0), axis=-1, keepdims=True) * (1.0 / 64)
            ms = jnp.where(sel, ssq, ms)
        co_ref[...] = d * lax.rsqrt(ms + EPS) * gsub_ref[...] * (1.0 - lam_init)


def _paged(page_table, cache_mla_t, cache_dk_t, cache_dv_t, layer, qlat, qr, dq, row, dkn, dvn,
           wuv, lq1, lk1, lq2, lk2, gsub4, *, nb, ts, lam_init):
    n_pages = page_table.shape[1]
    g = _tile(n_pages, 16)
    kern = functools.partial(_paged_kernel, g=g, ts=ts, lam_init=lam_init)

    def page_spec(feat, gi):
        return pl.BlockSpec((None, None, feat, PAGE),
                            lambda b, j, pt: (layer, pt[b, j * g + gi], 0, 0))

    per_b = lambda r, n: pl.BlockSpec((None, r, n), lambda b, j, pt: (b, 0, 0))
    vec = lambda a: pl.BlockSpec(a.shape, lambda b, j, pt: (0, 0))
    in_specs = ([page_spec(160, gi) for gi in range(g)] + [page_spec(256, gi) for gi in range(g)]
                + [page_spec(256, gi) for gi in range(g)]
                + [per_b(4 * ts, 128), per_b(4 * ts, 32), per_b(ts, 256), per_b(ts, 160),
                   per_b(ts, 256), per_b(ts, 256), vec(wuv), vec(lq1), vec(lk1), vec(lq2),
                   vec(lk2), vec(gsub4)])
    grid_spec = pltpu.PrefetchScalarGridSpec(
        num_scalar_prefetch=1,
        grid=(nb, n_pages // g),
        in_specs=in_specs,
        out_specs=[per_b(ts, 256), per_b(ts, 256)],
        scratch_shapes=[pltpu.VMEM((8 * ts, 256), F32),
                        pltpu.VMEM((4 * ts, 1), F32), pltpu.VMEM((4 * ts, 1), F32),
                        pltpu.VMEM((4 * ts, 128), F32),
                        pltpu.VMEM((8 * ts, 1), F32), pltpu.VMEM((8 * ts, 1), F32),
                        pltpu.VMEM((8 * ts, 256), F32)])
    return pl.pallas_call(
        kern,
        out_shape=[jax.ShapeDtypeStruct((nb, ts, 256), F32),
                   jax.ShapeDtypeStruct((nb, ts, 256), F32)],
        grid_spec=grid_spec,
        compiler_params=_cp(("parallel", "arbitrary")),
        name="paged_attn",
    )(page_table, *([cache_mla_t] * g), *([cache_dk_t] * g), *([cache_dv_t] * g),
      qlat, qr, dq, row, dkn, dvn, wuv, lq1, lk1, lq2, lk2, gsub4)


def _pack_w_in(w):
    d = w.shape[0]
    a_qkv, a_z, a_b, a_a, b_cq, b_ckv, b_kr, c_q, c_k, c_v = jnp.split(
        w, np.cumsum([1536, 512, 4, 4, 192, 128, 32, 256, 256]).tolist(), axis=1)
    z = lambda n: jnp.zeros((d, n), w.dtype)
    kr_sw = jnp.concatenate([-b_kr[:, 16:32], b_kr[:, 0:16]], axis=1)
    v4 = c_v.reshape(d, 4, 64)
    vdup = jnp.concatenate([v4, v4], axis=2).reshape(d, 512)
    return jnp.concatenate(
        [a_qkv, a_z, a_b, a_a, z(120), b_ckv, b_cq, z(64), jnp.tile(b_kr, (1, 4)),
         jnp.tile(kr_sw, (1, 4)), c_q, c_k, vdup], axis=1).astype(BF16)


def _pack_mla(w_uq, w_uk, w_uv):
    nope = w_uq[:, :, 0:64].reshape(192, 256)
    rope = w_uq[:, :, 64:96]
    rope_sw = jnp.concatenate([-rope[:, :, 16:32], rope[:, :, 0:16]], axis=2)
    wuq = jnp.concatenate([nope, rope.reshape(192, 128), rope_sw.reshape(192, 128)], axis=1)
    wuq = jnp.concatenate([wuq, jnp.zeros((64, 512), wuq.dtype)], axis=0)
    eye = jnp.eye(4, dtype=w_uk.dtype)
    wuk_bd = jnp.einsum('rhn,hg->hngr', w_uk, eye).reshape(256, 512)
    wuv_bd = jnp.einsum('rhe,hg->hrge', w_uv, eye).reshape(512, 256)
    return wuq.astype(BF16), wuk_bd.astype(BF16), wuv_bd.astype(BF16)


def _rope_tables(pos):
    freqs = jnp.power(ROPE_THETA, -jnp.arange(16, dtype=F32) / 16)
    ang = pos[:, None] * freqs[None, :]
    return jnp.tile(jnp.cos(ang), (1, 8)), jnp.tile(jnp.sin(ang), (1, 8))


def _lane_pad(v, n=LANES, offset=0):
    v = v.reshape(1, -1).astype(F32)
    return jnp.pad(v, ((0, 0), (offset, n - offset - v.shape[1])))


def kernel(x_prompt, x_sample, cache_mla, cache_diff_k, cache_diff_v, state_dn_S, state_dn_conv, page_table, g_f1_pre, g_f1_post, w_f1_gate, w_f1_up, w_f1_down, g_mix_pre, g_mix_post, w_in, w_out, dn_conv_w, dn_A_log, dn_dt_bias, dn_g_out, mla_g_q, mla_w_uq, mla_g_kv, mla_w_uk, mla_w_uv, diff_lam_q1, diff_lam_k1, diff_lam_q2, diff_lam_k2, diff_g_subln, g_f2_pre, g_f2_post, w_f2_gate, w_f2_up, w_f2_down):
    bp, tp, d = x_prompt.shape
    bs, ts, _ = x_sample.shape
    depth = w_in.shape[0]
    n_pages = page_table.shape[1]
    past_len = n_pages * PAGE
    mla_qscale = 96.0 ** -0.5 * LOG2E
    diff_qscale = 32.0 ** -0.5 * LOG2E

    cos_p, sin_p = _rope_tables(jnp.arange(tp, dtype=F32))
    cos_p, sin_p = jnp.tile(cos_p, (bp, 1)), jnp.tile(sin_p, (bp, 1))
    cos_s, sin_s = _rope_tables(past_len + jnp.arange(ts, dtype=F32))
    cos_s, sin_s = jnp.tile(cos_s, (bs, 1)), jnp.tile(sin_s, (bs, 1))

    n_pool = cache_mla.shape[1]
    cache_mla_t = jnp.swapaxes(cache_mla, 2, 3)
    cache_dk_t = jnp.transpose(cache_diff_k, (0, 1, 3, 4, 2)).reshape(depth, n_pool, 256, PAGE)
    cache_dv_t = jnp.transpose(cache_diff_v, (0, 1, 3, 4, 2)).reshape(depth, n_pool, 256, PAGE)
    zero_conv = jnp.zeros((1, bp, CONV_W - 1, 1536), F32)
    zero_s = jnp.zeros((1, bp, 4, 128, 128), F32)

    xp = x_prompt.reshape(bp * tp, d)
    xs = x_sample.reshape(bs * ts, d)
    p_out = [[] for _ in range(5)]
    s_out = [[] for _ in range(5)]
    row1 = lambda v: v.reshape(1, -1).astype(F32)
    for l in range(depth):
        lam_init = 0.8 - 0.6 * math.exp(-0.3 * l)
        wg1, wu1, wd1 = (w_f1_gate[l].astype(BF16), w_f1_up[l].astype(BF16),
                         w_f1_down[l].astype(BF16))
        wg2, wu2, wd2 = (w_f2_gate[l].astype(BF16), w_f2_up[l].astype(BF16),
                         w_f2_down[l].astype(BF16))
        w_inp = _pack_w_in(w_in[l])
        wuq, wuk_bd, wuv_bd = _pack_mla(mla_w_uq[l], mla_w_uk[l], mla_w_uv[l])
        w_o = w_out[l].astype(BF16)
        gq = _lane_pad(mla_g_q[l], 256)
        gkv = row1(mla_g_kv[l])
        alog = _lane_pad(dn_A_log[l], offset=4)
        dtb = _lane_pad(dn_dt_bias[l], offset=4)
        gout = row1(dn_g_out[l])
        cw = dn_conv_w[l].astype(F32)
        lq1, lk1, lq2, lk2 = (_lane_pad(diff_lam_q1[l]), _lane_pad(diff_lam_k1[l]),
                              _lane_pad(diff_lam_q2[l]), _lane_pad(diff_lam_k2[l]))
        gsub2 = jnp.tile(row1(diff_g_subln[l]), (1, 2))
        gsub4 = jnp.tile(row1(diff_g_subln[l]), (1, 4))

        def front(x, cos, sin):
            x = _ffn(x, row1(g_f1_pre[l]), wg1, wu1, wd1, row1(g_f1_post[l]))
            return x, _proj(x, row1(g_mix_pre[l]), w_inp, cos, sin, gq, wuq, wuk_bd, gkv,
                            mla_qscale=mla_qscale, diff_qscale=diff_qscale)

        def back(x, ao, bo, co):
            return _ffn(x, row1(g_f2_pre[l]), wg2, wu2, wd2, row1(g_f2_post[l]),
                        mix=(ao, bo, co, w_o, row1(g_mix_post[l])))

        xp, (qkv, z, ba, mrow, kp, qlat, qrope, dq, dk, dv, dkb, dvb) = front(xp, cos_p, sin_p)
        ao, s_new = _gdn(qkv, z, ba, zero_conv, zero_s, 0, cw, alog, dtb, gout, nb=bp, t=tp)
        bo = _mla_attn(qlat, qrope, kp, wuv_bd, nb=bp, t=tp)
        co = _diff_attn(dq, dkb, dvb, lq1, lk1, lq2, lk2, gsub2, nb=bp, t=tp, lam_init=lam_init)
        xp = back(xp, ao, bo, co)
        p_out[0].append(qkv.reshape(bp, tp, 1536)[:, tp - (CONV_W - 1):])
        p_out[1].append(s_new)
        p_out[2].append(mrow.reshape(bp, tp, 160))
        p_out[3].append(dk.reshape(bp, tp, 8, 32))
        p_out[4].append(dv.reshape(bp, tp, 4, 64))

        xs, (qkv, z, ba, mrow, kp, qlat, qrope, dq, dk, dv, dkb, dvb) = front(xs, cos_s, sin_s)
        ao, s_new = _gdn(qkv, z, ba, state_dn_conv, state_dn_S, l, cw, alog, dtb, gout,
                         nb=bs, t=ts)
        qlat_h = qlat.reshape(bs, ts, 4, 128).transpose(0, 2, 1, 3).reshape(bs, 4 * ts, 128)
        qr_h = qrope.reshape(bs, ts, 4, 32).transpose(0, 2, 1, 3).reshape(bs, 4 * ts, 32)
        bo, co = _paged(page_table, cache_mla_t, cache_dk_t, cache_dv_t, l, qlat_h, qr_h,
                        dq.reshape(bs, ts, 256).astype(F32), mrow.reshape(bs, ts, 160),
                        dk.reshape(bs, ts, 256), dv.reshape(bs, ts, 256), wuv_bd,
                        lq1, lk1, lq2, lk2, gsub4, nb=bs, ts=ts, lam_init=lam_init)
        xs = back(xs, ao, bo.reshape(bs * ts, 256), co.reshape(bs * ts, 256))
        s_out[0].append(qkv.reshape(bs, ts, 1536)[:, ts - (CONV_W - 1):])
        s_out[1].append(s_new)
        s_out[2].append(mrow.reshape(bs, ts, 160))
        s_out[3].append(dk.reshape(bs, ts, 8, 32))
        s_out[4].append(dv.reshape(bs, ts, 4, 64))

    p_conv, p_s, p_mla, p_dk, p_dv = [jnp.stack(a, axis=0) for a in p_out]
    s_conv, s_s, s_mla, s_dk, s_dv = [jnp.stack(a, axis=0) for a in s_out]
    return (xp.reshape(bp, tp, d), xs.reshape(bs, ts, d), p_mla, p_dk, p_dv, p_s, p_conv,
            s_mla, s_dk, s_dv, s_s, s_conv)
```

```python
import functools
import math

import jax
import jax.numpy as jnp
import numpy as np
from jax import lax
from jax.experimental import pallas as pl
from jax.experimental.pallas import tpu as pltpu

F32 = jnp.float32
BF16 = jnp.bfloat16
EPS = 1e-6
ROPE_THETA = 10000.0
CONV_W = 4
DN_CHUNK = 64
PAGE = 128
LANES = 128
LOG2E = math.log2(math.e)
MXU_N = 256
VMEM_LIMIT = 52 * 1024 * 1024
NEG = -1e30

_P_QKV, _P_Z, _P_BA, _P_CQ, _P_KR, _P_DQ, _P_DK, _P_DV, _P_END = (
    0, 1536, 2048, 2304, 2560, 2816, 3072, 3328, 3840)


def _cp(sem):
    return pltpu.CompilerParams(dimension_semantics=sem, vmem_limit_bytes=VMEM_LIMIT)


def _tile(m, pref):
    t = pref
    while m % t:
        t //= 2
    return t


def _mm(a, b):
    return jnp.dot(a.astype(BF16), b.astype(BF16), preferred_element_type=F32)


def _mm_nt(a, b):
    return lax.dot_general(a.astype(BF16), b.astype(BF16), (((1,), (1,)), ((), ())),
                           preferred_element_type=F32)


def _mm_tn(a, b):
    return lax.dot_general(a.astype(BF16), b.astype(BF16), (((0,), (0,)), ((), ())),
                           preferred_element_type=F32)


def _rms(x, g, n=None):
    n = x.shape[-1] if n is None else n
    ms = jnp.sum(x * x, axis=-1, keepdims=True) * (1.0 / n)
    return x * lax.rsqrt(ms + EPS) * g


def _silu(x):
    return x / (1.0 + jnp.exp(-x))


def _lane_group(lane, width, idx):
    return (lane >= idx * width) & (lane < (idx + 1) * width)


def _rep(x, k):
    return x if k == 1 else jnp.concatenate([x] * k, axis=1)


def _ffn_body(x, gpre_ref, wg_ref, wu_ref, wd_ref, gpost_ref, act_scr):
    h = _rms(x, gpre_ref[...]).astype(BF16)
    ff = wg_ref.shape[1]
    cw = MXU_N
    for c in range(ff // cw):
        a = jnp.dot(h, wg_ref[:, c * cw:(c + 1) * cw], preferred_element_type=F32)
        b = jnp.dot(h, wu_ref[:, c * cw:(c + 1) * cw], preferred_element_type=F32)
        act_scr[:, c * cw:(c + 1) * cw] = (_silu(a) * b).astype(BF16)
    y = jnp.dot(act_scr[...], wd_ref[...], preferred_element_type=F32)
    return x + 0.5 * _rms(y, gpost_ref[...])


def _ffn_kernel(x_ref, gpre_ref, wg_ref, wu_ref, wd_ref, gpost_ref, o_ref, act_scr):
    o_ref[...] = _ffn_body(x_ref[...], gpre_ref, wg_ref, wu_ref, wd_ref, gpost_ref, act_scr)


def _outproj_ffn_kernel(x_ref, ao_ref, bo_ref, co_ref, wo_ref, gmix_ref, gpre_ref, wg_ref, wu_ref,
                        wd_ref, gpost_ref, o_ref, act_scr):
    y = (_mm(ao_ref[...], wo_ref[0:512, :]) + _mm(bo_ref[...], wo_ref[512:768, :])
         + _mm(co_ref[...], wo_ref[768:1024, :]))
    x = x_ref[...] + _rms(y, gmix_ref[...])
    o_ref[...] = _ffn_body(x, gpre_ref, wg_ref, wu_ref, wd_ref, gpost_ref, act_scr)


def _resident(a):
    return pl.BlockSpec(a.shape, lambda i: (0,) * a.ndim, pipeline_mode=pl.Buffered(1))


def _ffn(x, gpre, wg, wu, wd, gpost, mix=None):
    m, d = x.shape
    ff = wg.shape[1]
    assert ff % MXU_N == 0
    tm = _tile(m, 512)
    row = lambda n: pl.BlockSpec((tm, n), lambda i: (i, 0))
    weights = [gpre, wg, wu, wd, gpost]
    if mix is None:
        kern, ins, in_specs = _ffn_kernel, [x], [row(d)]
    else:
        ao, bo, co, wo, gmix = mix
        kern, ins = _outproj_ffn_kernel, [x, ao, bo, co, wo, gmix]
        in_specs = [row(d), row(512), row(256), row(256), _resident(wo), _resident(gmix)]
    return pl.pallas_call(
        kern,
        out_shape=jax.ShapeDtypeStruct((m, d), F32),
        grid=(m // tm,),
        in_specs=in_specs + [_resident(w) for w in weights],
        out_specs=row(d),
        scratch_shapes=[pltpu.VMEM((tm, ff), BF16)],
        compiler_params=_cp(("parallel",)),
        name="ffn" if mix is None else "outproj_ffn",
    )(*ins, *weights)


def _proj_kernel(x_ref, g_ref, w_ref, cos_ref, sin_ref, gq_ref, wuq_ref, wuk_ref,
                 gkv_ref, qkv_ref, z_ref, ba_ref, row_ref, kp_ref, qlat_ref, qrope_ref,
                 dq_ref, dk_ref, dv_ref, dkb_ref, dvb_ref, *, mla_qscale, diff_qscale):
    h = _rms(x_ref[...], g_ref[...]).astype(BF16)

    def proj(lo, hi):
        return jnp.dot(h, w_ref[:, lo:hi], preferred_element_type=F32)

    qkv_ref[...] = proj(_P_QKV, _P_Z)
    z_ref[...] = proj(_P_Z, _P_BA)
    ba_ckv = proj(_P_BA, _P_CQ)
    ba_ref[...] = ba_ckv[:, 0:128]
    cos = cos_ref[...]
    sin = sin_ref[...]
    cq = _rms(proj(_P_CQ, _P_KR), gq_ref[...], n=192)
    qb = _mm(cq, wuq_ref[...])
    qlat_ref[...] = (_mm(qb[:, 0:256], wuk_ref[...]) * mla_qscale).astype(BF16)
    qrope_ref[...] = ((qb[:, 256:384] * cos + qb[:, 384:512] * sin) * mla_qscale).astype(BF16)
    ckv = _rms(ba_ckv[:, 128:256], gkv_ref[...])
    kr2 = proj(_P_KR, _P_DQ)
    kr = kr2[:, 0:128] * cos + kr2[:, 128:256] * sin
    row_ref[:, 0:128] = ckv
    row_ref[:, 128:160] = kr[:, 0:32]
    kp_ref[:, 0:128] = ckv.astype(BF16)
    kp_ref[:, 128:256] = kr.astype(BF16)
    dq_ref[...] = (proj(_P_DQ, _P_DK) * diff_qscale).astype(BF16)
    dk = proj(_P_DK, _P_DV)
    dk_ref[...] = dk
    dkb_ref[...] = dk.astype(BF16)
    dvd = proj(_P_DV, _P_END)
    dvb_ref[...] = dvd.astype(BF16)
    lane = lax.broadcasted_iota(jnp.int32, (dvd.shape[0], LANES), 1)
    for j in range(2):
        dv_ref[:, j * 128:(j + 1) * 128] = jnp.where(
            lane < 64, dvd[:, (2 * j) * 128:(2 * j + 1) * 128],
            dvd[:, (2 * j + 1) * 128:(2 * j + 2) * 128])


def _proj(x, g, w, cos, sin, gq, wuq, wuk, gkv, *, mla_qscale, diff_qscale):
    m, d = x.shape
    tm = _tile(m, 512)
    row = lambda n: pl.BlockSpec((tm, n), lambda i: (i, 0))
    full = _resident
    outs = [(1536, F32), (512, F32), (128, F32), (160, F32), (256, BF16), (512, BF16),
            (128, BF16), (256, BF16), (256, F32), (256, F32), (256, BF16), (512, BF16)]
    kern = functools.partial(_proj_kernel, mla_qscale=mla_qscale, diff_qscale=diff_qscale)
    return pl.pallas_call(
        kern,
        out_shape=[jax.ShapeDtypeStruct((m, n), dt) for n, dt in outs],
        grid=(m // tm,),
        in_specs=[row(d), full(g), full(w), row(128), row(128), full(gq), full(wuq),
                  full(wuk), full(gkv)],
        out_specs=[row(n) for n, _ in outs],
        compiler_params=_cp(("parallel",)),
        name="proj_in",
    )(x, g, w, cos, sin, gq, wuq, wuk, gkv)


def _gdn_kernel(x_ref, z_ref, ba_ref, conv0_ref, s0_ref, cw_ref, alog_ref, dtb_ref, gout_ref,
                o_ref, sout_ref, xs_scr, s_scr, *, nbb, t_in, t_cmp, t_valid):
    C = DN_CHUNK
    nc = t_cmp // C
    tb = pl.program_id(1)

    @pl.when(tb == 0)
    def _():
        s_scr[...] = s0_ref[...]
        for bb in range(nbb):
            xs_scr[bb, 5:8, :] = conv0_ref[bb]

    @pl.when(tb != 0)
    def _():
        for bb in range(nbb):
            xs_scr[bb, 5:8, :] = xs_scr[bb, 5 + t_in:8 + t_in, :]

    for bb in range(nbb):
        xs_scr[bb, 8:8 + t_in, :] = x_ref[bb * t_in:(bb + 1) * t_in, :]
        if t_cmp > t_in:
            xs_scr[bb, 8 + t_in:8 + t_cmp, :] = jnp.zeros((t_cmp - t_in, xs_scr.shape[2]), F32)

    ri = lax.broadcasted_iota(jnp.int32, (C, C), 0)
    ci = lax.broadcasted_iota(jnp.int32, (C, C), 1)
    incl = ri >= ci
    strict = ri > ci
    tril = jnp.where(incl, 1.0, 0.0).astype(F32)
    alog = alog_ref[...]
    dtb = dtb_ref[...]
    gout = gout_ref[...]

    trip = []
    for bb in range(nbb):
        for c in range(nc):
            r0 = c * C
            y = xs_scr[bb, 5 + r0:5 + r0 + C, :] * cw_ref[0:1, :]
            for i in range(1, CONV_W):
                y = y + xs_scr[bb, 5 + r0 + i:5 + r0 + i + C, :] * cw_ref[i:i + 1, :]
            qkv = _silu(y)
            if t_in >= t_cmp:
                ba = ba_ref[bb * t_in + r0:bb * t_in + r0 + C, :]
            else:
                ba = jnp.concatenate([ba_ref[bb * t_in:(bb + 1) * t_in, :],
                                      jnp.zeros((t_cmp - t_in, LANES), F32)], axis=0)
            bet_all = 1.0 / (1.0 + jnp.exp(-ba))
            xa = ba + dtb
            g_all = -jnp.exp(alog) * (jnp.maximum(xa, 0.0)
                                      + jnp.log(1.0 + jnp.exp(-jnp.abs(xa))))
            if t_valid < t_cmp:
                valid = lax.broadcasted_iota(jnp.int32, (C, 1), 0) + r0 < t_valid
                qkv = jnp.where(valid, qkv, 0.0)
                bet_all = jnp.where(valid, bet_all, 0.0)
                g_all = jnp.where(valid, g_all, 0.0)
            gam_all = jnp.dot(tril, g_all, preferred_element_type=F32,
                              precision=lax.Precision.HIGHEST)
            gam_t = gam_all.T
            for h in range(4):
                q = qkv[:, h * 128:(h + 1) * 128]
                k = qkv[:, 512 + h * 128:512 + (h + 1) * 128]
                v = qkv[:, 1024 + h * 128:1024 + (h + 1) * 128]
                q = q * lax.rsqrt(jnp.sum(q * q, axis=-1, keepdims=True) + EPS) * (128.0 ** -0.5)
                k = k * lax.rsqrt(jnp.sum(k * k, axis=-1, keepdims=True) + EPS)
                bet = bet_all[:, h:h + 1]
                gcol = gam_all[:, 4 + h:5 + h]
                grow = gam_t[4 + h:5 + h, :]
                glast = gam_all[C - 1:C, 4 + h:5 + h]
                diff = gcol - grow
                decay = jnp.where(incl, jnp.exp(jnp.where(incl, diff, 0.0)), 0.0)
                eg = jnp.exp(gcol)
                trip.append(dict(bb=bb, c=c, h=h, q=q, k=k, v=v, bet=bet, decay=decay, eg=eg,
                                 kd=k * jnp.exp(glast - gcol), sdec=jnp.exp(glast)))

    for t in trip:
        t["kk"] = _mm_nt(t["k"], t["k"])
    for t in trip:
        t["qk"] = _mm_nt(t["q"], t["k"]) * t["decay"]
    for t in trip:
        t["tp"] = jnp.where(strict, t["bet"] * t["kk"] * t["decay"], 0.0)
        t["x"] = jnp.concatenate([t["v"] * t["bet"], t["k"] * (t["bet"] * t["eg"])], axis=-1)
    b16r, b16c = lax.shift_right_logical(ri, 4), lax.shift_right_logical(ci, 4)
    b32r, b32c = lax.shift_right_logical(ri, 5), lax.shift_right_logical(ci, 5)
    in16 = b16r == b16c
    in32 = b32r == b32c
    eye = jnp.where(ri == ci, 1.0, 0.0).astype(F32)
    for t in trip:
        t["td"] = jnp.where(in16, t["tp"], 0.0)
    for t in trip:
        t["s2"] = _mm(t["td"], t["td"])
    for t in trip:
        t["s4"] = _mm(t["s2"], t["s2"])
        t["a"] = eye - t["td"] + t["s2"] - _mm(t["td"], t["s2"])
    for t in trip:
        t["s8"] = _mm(t["s4"], t["s4"])
    for t in trip:
        t["bm"] = t["s4"] + t["s8"] + _mm(t["s4"], t["s8"])
    for t in trip:
        t["p"] = t["a"] + _mm(t["a"], t["bm"])
    for t in trip:
        t["lp"] = _mm(jnp.where(in32 & ~in16, t["tp"], 0.0), t["p"])
    for t in trip:
        t["p"] = t["p"] - _mm(t["p"], t["lp"])
    for t in trip:
        t["lp"] = _mm(jnp.where(in32, 0.0, t["tp"]), t["p"])
    for t in trip:
        t["p"] = t["p"] - _mm(t["p"], t["lp"])
    for t in trip:
        t["x"] = _mm(t["p"], t["x"])
    state = {(bb, h): s_scr[bb, h] for bb in range(nbb) for h in range(4)}
    for c in range(nc):
        cur = [t for t in trip if t["c"] == c]
        for t in cur:
            s_h = state[(t["bb"], t["h"])]
            t["ws"] = _mm(t["x"][:, 128:256], s_h)
            t["qs"] = _mm(t["q"], s_h)
        for t in cur:
            t["u"] = t["x"][:, 0:128] - t["ws"]
        for t in cur:
            t["o"] = t["eg"] * t["qs"] + _mm(t["qk"], t["u"])
            key = (t["bb"], t["h"])
            state[key] = t["sdec"] * state[key] + _mm_tn(t["kd"], t["u"])
        r0 = c * C
        if r0 < t_in:
            n = min(C, t_in - r0)
            for t in cur:
                bb, h = t["bb"], t["h"]
                zz = z_ref[bb * t_in + r0:bb * t_in + r0 + n, h * 128:(h + 1) * 128]
                o_ref[bb * t_in + r0:bb * t_in + r0 + n, h * 128:(h + 1) * 128] = (
                    _rms(t["o"][0:n, :], gout) * _silu(zz)).astype(o_ref.dtype)
    for (bb, h), s_h in state.items():
        s_scr[bb, h] = s_h

    @pl.when(tb == pl.num_programs(1) - 1)
    def _():
        sout_ref[...] = s_scr[...]


def _gdn(qkv, z, ba, conv0, s0, layer, cw, alog, dtb, gout, *, nb, t):
    if t >= 128:
        nbb = 1
        t_in = t_cmp = t_valid = _tile(t, 256)
    else:
        nbb = _tile(nb, 4)
        t_in, t_cmp, t_valid = t, DN_CHUNK, t
    nt = t // t_in
    rows = lambda n: pl.BlockSpec((nbb * t_in, n), lambda b, j: (b * nt + j, 0))
    full = lambda a: pl.BlockSpec(a.shape, lambda b, j: (0,) * a.ndim)
    kern = functools.partial(_gdn_kernel, nbb=nbb, t_in=t_in, t_cmp=t_cmp, t_valid=t_valid)
    return pl.pallas_call(
        kern,
        out_shape=[jax.ShapeDtypeStruct((nb * t, 512), BF16 if t_in % 16 == 0 else F32),
                   jax.ShapeDtypeStruct((nb, 4, 128, 128), F32)],
        grid=(nb // nbb, nt),
        in_specs=[rows(1536), rows(512), rows(128),
                  pl.BlockSpec((None, nbb, 3, 1536), lambda b, j: (layer, b, 0, 0)),
                  pl.BlockSpec((None, nbb, 4, 128, 128), lambda b, j: (layer, b, 0, 0, 0)),
                  full(cw), full(alog), full(dtb), full(gout)],
        out_specs=[rows(512), pl.BlockSpec((nbb, 4, 128, 128), lambda b, j: (b, 0, 0, 0))],
        scratch_shapes=[pltpu.VMEM((nbb, 8 + t_cmp, 1536), F32),
                        pltpu.VMEM((nbb, 4, 128, 128), F32)],
        compiler_params=_cp(("parallel", "arbitrary")),
        name="gdn",
    )(qkv, z, ba, conv0, s0, cw, alog, dtb, gout)


def _causal_sweep(i, tq, tk, nheads, step):
    n_full = (i * tq) // tk
    n_tot = ((i + 1) * tq + tk - 1) // tk

    def full_body(j, carry):
        step(j, None)
        return carry

    lax.fori_loop(0, n_full, full_body, 0)

    def diag_body(j, carry):
        qpos = i * tq + lax.broadcasted_iota(jnp.int32, (tq, tk), 0)
        kpos = j * tk + lax.broadcasted_iota(jnp.int32, (tq, tk), 1)
        mask = kpos <= qpos
        step(j, jnp.concatenate([mask] * nheads, axis=0))
        return carry

    lax.fori_loop(n_full, n_tot, diag_body, 0)


def _online_softmax(s, m_scr, tk):
    m_prev = m_scr[...]
    m_next = jnp.maximum(m_prev, jnp.max(s, axis=-1, keepdims=True))
    p = jnp.exp2(s - _rep(m_next, tk // LANES)).astype(BF16)
    alpha = jnp.exp2(m_prev - m_next)
    m_scr[...] = m_next
    return p, alpha


def _mla_attn_kernel(qlat_ref, qrope_ref, kp_ref, wuv_ref, o_ref, q_scr, m_scr, acc_scr,
                     *, tq, tk):
    i = pl.program_id(1)
    qr = qrope_ref[...]
    lane = lax.broadcasted_iota(jnp.int32, qr.shape, 1)
    for h in range(4):
        q_scr[h * tq:(h + 1) * tq, 0:128] = qlat_ref[:, h * 128:(h + 1) * 128]
        q_scr[h * tq:(h + 1) * tq, 128:256] = jnp.where(_lane_group(lane, 32, h), qr,
                                                        jnp.zeros_like(qr))
    m_scr[...] = jnp.full(m_scr.shape, -jnp.inf, F32)
    acc_scr[...] = jnp.zeros(acc_scr.shape, F32)
    ones = jnp.ones((tk, LANES), BF16)

    def step(j, mask):
        kc = kp_ref[pl.ds(pl.multiple_of(j * tk, tk), tk), :]
        s = lax.dot_general(q_scr[...], kc, (((1,), (1,)), ((), ())),
                            preferred_element_type=F32)
        if mask is not None:
            s = jnp.where(mask, s, NEG)
        p, alpha = _online_softmax(s, m_scr, tk)
        vext = jnp.concatenate([kc[:, 0:128], ones], axis=1)
        acc_scr[...] = acc_scr[...] * _rep(alpha, 2) + jnp.dot(
            p, vext, preferred_element_type=F32)

    _causal_sweep(i, tq, tk, 4, step)
    o = (acc_scr[:, 0:128] / acc_scr[:, 128:256]).astype(BF16)
    bo = jnp.dot(o[0:tq], wuv_ref[0:128, :], preferred_element_type=F32)
    for h in range(1, 4):
        bo = bo + jnp.dot(o[h * tq:(h + 1) * tq], wuv_ref[h * 128:(h + 1) * 128, :],
                          preferred_element_type=F32)
    o_ref[...] = bo.astype(BF16)


def _mla_attn(qlat, qrope, kp, wuv, *, nb, t):
    tq = _tile(t, 512)
    tk = _tile(t, 512)
    nq = t // tq
    kern = functools.partial(_mla_attn_kernel, tq=tq, tk=tk)
    return pl.pallas_call(
        kern,
        out_shape=jax.ShapeDtypeStruct((nb * t, 256), BF16),
        grid=(nb, nq),
        in_specs=[pl.BlockSpec((tq, 512), lambda b, i: (b * nq + i, 0)),
                  pl.BlockSpec((tq, 128), lambda b, i: (b * nq + i, 0)),
                  pl.BlockSpec((t, 256), lambda b, i: (b, 0)),
                  pl.BlockSpec(wuv.shape, lambda b, i: (0, 0))],
        out_specs=pl.BlockSpec((tq, 256), lambda b, i: (b * nq + i, 0)),
        scratch_shapes=[pltpu.VMEM((4 * tq, 256), BF16), pltpu.VMEM((4 * tq, LANES), F32),
                        pltpu.VMEM((4 * tq, 256), F32)],
        compiler_params=_cp(("parallel", "parallel")),
        name="mla_attn",
    )(qlat, qrope, kp, wuv)


def _diff_lambda(lq1, lk1, lq2, lk2, lam_init):
    return (jnp.exp(jnp.sum(lq1 * lk1, axis=-1, keepdims=True))
            - jnp.exp(jnp.sum(lq2 * lk2, axis=-1, keepdims=True)) + lam_init)


def _diff_attn_kernel(q_ref, k_ref, v_ref, lq1_ref, lk1_ref, lq2_ref, lk2_ref, gsub_ref, o_ref,
                      q_scr, m_scr, acc_scr, *, tq, tk, lam_init):
    i = pl.program_id(1)
    q = q_ref[...]
    lane = lax.broadcasted_iota(jnp.int32, q.shape, 1)
    for h in range(8):
        q_scr[h * tq:(h + 1) * tq, :] = jnp.where(_lane_group(lane, 32, h), q, jnp.zeros_like(q))
    m_scr[...] = jnp.full(m_scr.shape, -jnp.inf, F32)
    acc_scr[...] = jnp.zeros(acc_scr.shape, F32)
    ones = jnp.ones((tk, LANES), BF16)

    def step(j, mask):
        start = pl.multiple_of(j * tk, tk)
        kc = k_ref[pl.ds(start, tk), :]
        s = lax.dot_general(q_scr[...], kc, (((1,), (1,)), ((), ())),
                            preferred_element_type=F32)
        if mask is not None:
            s = jnp.where(mask, s, NEG)
        p, alpha = _online_softmax(s, m_scr, tk)
        alpha = _rep(alpha, 2)
        for c in range(4):
            vext = jnp.concatenate([v_ref[pl.ds(start, tk), c * 128:(c + 1) * 128], ones], axis=1)
            r = slice(2 * c * tq, (2 * c + 2) * tq)
            acc_scr[r, :] = acc_scr[r, :] * alpha[r, :] + jnp.dot(
                p[r, :], vext, preferred_element_type=F32)

    _causal_sweep(i, tq, tk, 8, step)
    lam = _diff_lambda(lq1_ref[...], lk1_ref[...], lq2_ref[...], lk2_ref[...], lam_init)
    on = acc_scr[:, 0:128] / acc_scr[:, 128:256]
    lane128 = lax.broadcasted_iota(jnp.int32, (tq, LANES), 1)
    outs = []
    for c in range(4):
        d = on[2 * c * tq:(2 * c + 1) * tq] - lam * on[(2 * c + 1) * tq:(2 * c + 2) * tq]
        outs.append(_rms(d, gsub_ref[...]) * (1.0 - lam_init))
    for j in range(2):
        o_ref[:, j * 128:(j + 1) * 128] = jnp.where(
            lane128 < 64, outs[2 * j], outs[2 * j + 1]).astype(BF16)


def _diff_attn(q, k, vdup, lq1, lk1, lq2, lk2, gsub, *, nb, t, lam_init):
    tq = _tile(t, 256)
    tk = _tile(t, 512)
    nq = t // tq
    kern = functools.partial(_diff_attn_kernel, tq=tq, tk=tk, lam_init=lam_init)
    vec = lambda a: pl.BlockSpec(a.shape, lambda b, i: (0, 0))
    return pl.pallas_call(
        kern,
        out_shape=jax.ShapeDtypeStruct((nb * t, 256), BF16),
        grid=(nb, nq),
        in_specs=[pl.BlockSpec((tq, 256), lambda b, i: (b * nq + i, 0)),
                  pl.BlockSpec((t, 256), lambda b, i: (b, 0)),
                  pl.BlockSpec((t, 512), lambda b, i: (b, 0)),
                  vec(lq1), vec(lk1), vec(lq2), vec(lk2), vec(gsub)],
        out_specs=pl.BlockSpec((tq, 256), lambda b, i: (b * nq + i, 0)),
        scratch_shapes=[pltpu.VMEM((8 * tq, 256), BF16), pltpu.VMEM((8 * tq, LANES), F32),
                        pltpu.VMEM((8 * tq, 256), F32)],
        compiler_params=_cp(("parallel", "parallel")),
        name="diff_attn",
    )(q, k, vdup, lq1, lk1, lq2, lk2, gsub)


def _paged_kernel(pt_ref, mla_hbm, dk_hbm, dv_hbm, qlat_ref, qr_ref, dq_ref, row_ref, dkn_ref,
                  dvn_ref, wuv_ref, lq1_ref, lk1_ref, lq2_ref, lk2_ref, gsub_ref, bo_ref, co_ref,
                  mla_buf, dk_buf, dv_buf, sem, dqb_scr, ma_scr, la_scr, aa_scr, md_scr, ld_scr,
                  ad_scr, *, g, ts, layer, lam_init):
    b = pl.program_id(0)
    j = pl.program_id(1)
    nb = pl.num_programs(0)
    nj = pl.num_programs(1)

    def copies(seq, group, slot):
        out = []
        for gi in range(g):
            pg = pt_ref[seq, group * g + gi]
            for hbm, buf in ((mla_hbm, mla_buf), (dk_hbm, dk_buf), (dv_hbm, dv_buf)):
                out.append(pltpu.make_async_copy(hbm.at[layer, pg], buf.at[slot, gi],
                                                 sem.at[slot]))
        return out

    first = (b == 0) & (j == 0)
    last = (b == nb - 1) & (j == nj - 1)
    wrap = j == nj - 1
    nxt_b = jnp.where(last, b, jnp.where(wrap, b + 1, b))
    nxt_j = jnp.where(wrap, 0, j + 1)

    @pl.when(first)
    def _():
        for c in copies(b, 2 * j, 0):
            c.start()

    @pl.when(j == 0)
    def _():
        q = dq_ref[...]
        lane = lax.broadcasted_iota(jnp.int32, q.shape, 1)
        for h in range(8):
            dqb_scr[h * ts:(h + 1) * ts, :] = jnp.where(_lane_group(lane, 32, h), q, 0.0)
        ma_scr[...] = jnp.full(ma_scr.shape, -jnp.inf, F32)
        la_scr[...] = jnp.zeros(la_scr.shape, F32)
        aa_scr[...] = jnp.zeros(aa_scr.shape, F32)
        md_scr[...] = jnp.full(md_scr.shape, -jnp.inf, F32)
        ld_scr[...] = jnp.zeros(ld_scr.shape, F32)
        ad_scr[...] = jnp.zeros(ad_scr.shape, F32)

    qlat = qlat_ref[...].astype(BF16)
    qr = qr_ref[...].astype(BF16)
    dqb = dqb_scr[...].astype(BF16)

    def update(s_list, pv, m_scr, l_scr, acc_scr):
        s = jnp.concatenate(s_list, axis=-1) if len(s_list) > 1 else s_list[0]
        m_old = m_scr[...]
        m_new = jnp.maximum(m_old, jnp.max(s, axis=-1, keepdims=True))
        p = jnp.exp2(s - m_new)
        corr = jnp.exp2(m_old - m_new)
        l_scr[...] = l_scr[...] * corr + jnp.sum(p, axis=-1, keepdims=True)
        pb = p.astype(BF16)
        acc = acc_scr[...] * corr
        off = 0
        for idx, sb in enumerate(s_list):
            n = sb.shape[1]
            acc = acc + pv(pb[:, off:off + n], idx)
            off += n
        acc_scr[...] = acc
        m_scr[...] = m_new

    def consume(slot):
        kts = [mla_buf[slot, gi].astype(BF16) for gi in range(g)]
        sa = [jnp.dot(qlat, kt[0:128, :], preferred_element_type=F32)
              + jnp.dot(qr, kt[128:160, :], preferred_element_type=F32) for kt in kts]
        update(sa, lambda pb, idx: _mm_nt(pb, kts[idx][0:128, :]), ma_scr, la_scr, aa_scr)
        sd = [jnp.dot(dqb, dk_buf[slot, gi].astype(BF16), preferred_element_type=F32)
              for gi in range(g)]
        update(sd, lambda pb, idx: _mm_nt(pb, dv_buf[slot, idx]), md_scr, ld_scr, ad_scr)

    for c in copies(b, 2 * j + 1, 1):
        c.start()
    for c in copies(b, 2 * j, 0):
        c.wait()
    consume(0)
    for c in copies(nxt_b, 2 * nxt_j, 0):
        c.start()
    for c in copies(b, 2 * j + 1, 1):
        c.wait()
    consume(1)

    @pl.when(last)
    def _():
        for c in copies(nxt_b, 2 * nxt_j, 0):
            c.wait()

    @pl.when(wrap)
    def _():
        tsp = -(-ts // 16) * 16
        qpos = lax.broadcasted_iota(jnp.int32, (ts, tsp), 0)
        kpos = lax.broadcasted_iota(jnp.int32, (ts, tsp), 1)
        mask = kpos <= qpos

        def padded(ref):
            rows = ref[...]
            return jnp.concatenate([rows, jnp.zeros((tsp - ts, rows.shape[1]), F32)],
                                   axis=0).astype(BF16)

        rows = padded(row_ref)
        s = _mm_nt(qlat, rows[:, 0:128]) + _mm_nt(qr, rows[:, 128:160])
        update([jnp.where(jnp.concatenate([mask] * 4, axis=0), s, NEG)],
               lambda pb, idx: jnp.dot(pb, rows[:, 0:128], preferred_element_type=F32),
               ma_scr, la_scr, aa_scr)
        s = _mm_nt(dqb, padded(dkn_ref))
        vn = padded(dvn_ref)
        update([jnp.where(jnp.concatenate([mask] * 8, axis=0), s, NEG)],
               lambda pb, idx: jnp.dot(pb, vn, preferred_element_type=F32),
               md_scr, ld_scr, ad_scr)
        o = (aa_scr[...] / la_scr[...]).astype(BF16)
        bo = jnp.zeros((ts, 256), F32)
        for h in range(4):
            full = jnp.dot(o, wuv_ref[h * 128:(h + 1) * 128, :], preferred_element_type=F32)
            bo = bo + full[h * ts:(h + 1) * ts]
        bo_ref[...] = bo
        on = ad_scr[...] / ld_scr[...]
        lam = _diff_lambda(lq1_ref[...], lk1_ref[...], lq2_ref[...], lk2_ref[...], lam_init)
        lane = lax.broadcasted_iota(jnp.int32, (ts, 256), 1)
        d = jnp.zeros((ts, 256), F32)
        for c in range(4):
            dc = on[2 * c * ts:(2 * c + 1) * ts] - lam * on[(2 * c + 1) * ts:(2 * c + 2) * ts]
            d = jnp.where(_lane_group(lane, 64, c), dc, d)
        ms = jnp.zeros((ts, 256), F32)
        for c in range(4):
            sel = _lane_group(lane, 64, c)
            ssq = jnp.sum(jnp.where(sel, d * d, 0.0), axis=-1, keepdims=True) * (1.0 / 64)
            ms = jnp.where(sel, ssq, ms)
        co_ref[...] = d * lax.rsqrt(ms + EPS) * gsub_ref[...] * (1.0 - lam_init)


def _paged(page_table, cache_mla_t, cache_dk_t, cache_dv_t, layer, qlat, qr, dq, row, dkn, dvn,
           wuv, lq1, lk1, lq2, lk2, gsub4, *, nb, ts, lam_init):
    n_pages = page_table.shape[1]
    assert n_pages % 2 == 0
    g = _tile(n_pages // 2, 16)
    kern = functools.partial(_paged_kernel, g=g, ts=ts, layer=layer, lam_init=lam_init)
    per_b = lambda r, n: pl.BlockSpec((None, r, n), lambda b, j, pt: (b, 0, 0))
    vec = lambda a: pl.BlockSpec(a.shape, lambda b, j, pt: (0, 0))
    hbm = pl.BlockSpec(memory_space=pl.ANY)
    in_specs = [hbm, hbm, hbm,
                per_b(4 * ts, 128), per_b(4 * ts, 32), per_b(ts, 256), per_b(ts, 160),
                per_b(ts, 256), per_b(ts, 256), vec(wuv), vec(lq1), vec(lk1), vec(lq2),
                vec(lk2), vec(gsub4)]
    grid_spec = pltpu.PrefetchScalarGridSpec(
        num_scalar_prefetch=1,
        grid=(nb, n_pages // (2 * g)),
        in_specs=in_specs,
        out_specs=[per_b(ts, 256), per_b(ts, 256)],
        scratch_shapes=[pltpu.VMEM((2, g, 160, PAGE), F32), pltpu.VMEM((2, g, 256, PAGE), F32),
                        pltpu.VMEM((2, g, 256, PAGE), F32), pltpu.SemaphoreType.DMA((2,)),
                        pltpu.VMEM((8 * ts, 256), F32),
                        pltpu.VMEM((4 * ts, 1), F32), pltpu.VMEM((4 * ts, 1), F32),
                        pltpu.VMEM((4 * ts, 128), F32),
                        pltpu.VMEM((8 * ts, 1), F32), pltpu.VMEM((8 * ts, 1), F32),
                        pltpu.VMEM((8 * ts, 256), F32)])
    return pl.pallas_call(
        kern,
        out_shape=[jax.ShapeDtypeStruct((nb, ts, 256), F32),
                   jax.ShapeDtypeStruct((nb, ts, 256), F32)],
        grid_spec=grid_spec,
        compiler_params=_cp(("arbitrary", "arbitrary")),
        name="paged_attn",
    )(page_table, cache_mla_t, cache_dk_t, cache_dv_t,
      qlat, qr, dq, row, dkn, dvn, wuv, lq1, lk1, lq2, lk2, gsub4)


def _pack_w_in(w):
    d = w.shape[0]
    a_qkv, a_z, a_b, a_a, b_cq, b_ckv, b_kr, c_q, c_k, c_v = jnp.split(
        w, np.cumsum([1536, 512, 4, 4, 192, 128, 32, 256, 256]).tolist(), axis=1)
    z = lambda n: jnp.zeros((d, n), w.dtype)
    kr_sw = jnp.concatenate([-b_kr[:, 16:32], b_kr[:, 0:16]], axis=1)
    v4 = c_v.reshape(d, 4, 64)
    vdup = jnp.concatenate([v4, v4], axis=2).reshape(d, 512)
    return jnp.concatenate(
        [a_qkv, a_z, a_b, a_a, z(120), b_ckv, b_cq, z(64), jnp.tile(b_kr, (1, 4)),
         jnp.tile(kr_sw, (1, 4)), c_q, c_k, vdup], axis=1).astype(BF16)


def _pack_mla(w_uq, w_uk, w_uv):
    nope = w_uq[:, :, 0:64].reshape(192, 256)
    rope = w_uq[:, :, 64:96]
    rope_sw = jnp.concatenate([-rope[:, :, 16:32], rope[:, :, 0:16]], axis=2)
    wuq = jnp.concatenate([nope, rope.reshape(192, 128), rope_sw.reshape(192, 128)], axis=1)
    wuq = jnp.concatenate([wuq, jnp.zeros((64, 512), wuq.dtype)], axis=0)
    eye = jnp.eye(4, dtype=w_uk.dtype)
    wuk_bd = jnp.einsum('rhn,hg->hngr', w_uk, eye).reshape(256, 512)
    wuv_bd = jnp.einsum('rhe,hg->hrge', w_uv, eye).reshape(512, 256)
    return wuq.astype(BF16), wuk_bd.astype(BF16), wuv_bd.astype(BF16)


def _rope_tables(pos):
    freqs = jnp.power(ROPE_THETA, -jnp.arange(16, dtype=F32) / 16)
    ang = pos[:, None] * freqs[None, :]
    return jnp.tile(jnp.cos(ang), (1, 8)), jnp.tile(jnp.sin(ang), (1, 8))


def _lane_pad(v, n=LANES, offset=0):
    v = v.reshape(1, -1).astype(F32)
    return jnp.pad(v, ((0, 0), (offset, n - offset - v.shape[1])))


def kernel(x_prompt, x_sample, cache_mla, cache_diff_k, cache_diff_v, state_dn_S, state_dn_conv, page_table, g_f1_pre, g_f1_post, w_f1_gate, w_f1_up, w_f1_down, g_mix_pre, g_mix_post, w_in, w_out, dn_conv_w, dn_A_log, dn_dt_bias, dn_g_out, mla_g_q, mla_w_uq, mla_g_kv, mla_w_uk, mla_w_uv, diff_lam_q1, diff_lam_k1, diff_lam_q2, diff_lam_k2, diff_g_subln, g_f2_pre, g_f2_post, w_f2_gate, w_f2_up, w_f2_down):
    bp, tp, d = x_prompt.shape
    bs, ts, _ = x_sample.shape
    depth = w_in.shape[0]
    n_pages = page_table.shape[1]
    past_len = n_pages * PAGE
    mla_qscale = 96.0 ** -0.5 * LOG2E
    diff_qscale = 32.0 ** -0.5 * LOG2E

    cos_p, sin_p = _rope_tables(jnp.arange(tp, dtype=F32))
    cos_p, sin_p = jnp.tile(cos_p, (bp, 1)), jnp.tile(sin_p, (bp, 1))
    cos_s, sin_s = _rope_tables(past_len + jnp.arange(ts, dtype=F32))
    cos_s, sin_s = jnp.tile(cos_s, (bs, 1)), jnp.tile(sin_s, (bs, 1))

    n_pool = cache_mla.shape[1]
    cache_mla_t = jnp.swapaxes(cache_mla, 2, 3)
    cache_dk_t = jnp.transpose(cache_diff_k, (0, 1, 3, 4, 2)).reshape(depth, n_pool, 256, PAGE)
    cache_dv_t = jnp.transpose(cache_diff_v, (0, 1, 3, 4, 2)).reshape(depth, n_pool, 256, PAGE)
    zero_conv = jnp.zeros((1, bp, CONV_W - 1, 1536), F32)
    zero_s = jnp.zeros((1, bp, 4, 128, 128), F32)

    xp = x_prompt.reshape(bp * tp, d)
    xs = x_sample.reshape(bs * ts, d)
    p_out = [[] for _ in range(5)]
    s_out = [[] for _ in range(5)]
    row1 = lambda v: v.reshape(1, -1).astype(F32)
    for l in range(depth):
        lam_init = 0.8 - 0.6 * math.exp(-0.3 * l)
        wg1, wu1, wd1 = (w_f1_gate[l].astype(BF16), w_f1_up[l].astype(BF16),
                         w_f1_down[l].astype(BF16))
        wg2, wu2, wd2 = (w_f2_gate[l].astype(BF16), w_f2_up[l].astype(BF16),
                         w_f2_down[l].astype(BF16))
        w_inp = _pack_w_in(w_in[l])
        wuq, wuk_bd, wuv_bd = _pack_mla(mla_w_uq[l], mla_w_uk[l], mla_w_uv[l])
        w_o = w_out[l].astype(BF16)
        gq = _lane_pad(mla_g_q[l], 256)
        gkv = row1(mla_g_kv[l])
        alog = _lane_pad(dn_A_log[l], offset=4)
        dtb = _lane_pad(dn_dt_bias[l], offset=4)
        gout = row1(dn_g_out[l])
        cw = dn_conv_w[l].astype(F32)
        lq1, lk1, lq2, lk2 = (_lane_pad(diff_lam_q1[l]), _lane_pad(diff_lam_k1[l]),
                              _lane_pad(diff_lam_q2[l]), _lane_pad(diff_lam_k2[l]))
        gsub2 = jnp.tile(row1(diff_g_subln[l]), (1, 2))
        gsub4 = jnp.tile(row1(diff_g_subln[l]), (1, 4))

        def front(x, cos, sin):
            x = _ffn(x, row1(g_f1_pre[l]), wg1, wu1, wd1, row1(g_f1_post[l]))
            return x, _proj(x, row1(g_mix_pre[l]), w_inp, cos, sin, gq, wuq, wuk_bd, gkv,
                            mla_qscale=mla_qscale, diff_qscale=diff_qscale)

        def back(x, ao, bo, co):
            return _ffn(x, row1(g_f2_pre[l]), wg2, wu2, wd2, row1(g_f2_post[l]),
                        mix=(ao, bo, co, w_o, row1(g_mix_post[l])))

        xp, (qkv, z, ba, mrow, kp, qlat, qrope, dq, dk, dv, dkb, dvb) = front(xp, cos_p, sin_p)
        ao, s_new = _gdn(qkv, z, ba, zero_conv, zero_s, 0, cw, alog, dtb, gout, nb=bp, t=tp)
        bo = _mla_attn(qlat, qrope, kp, wuv_bd, nb=bp, t=tp)
        co = _diff_attn(dq, dkb, dvb, lq1, lk1, lq2, lk2, gsub2, nb=bp, t=tp, lam_init=lam_init)
        xp = back(xp, ao, bo, co)
        p_out[0].append(qkv.reshape(bp, tp, 1536)[:, tp - (CONV_W - 1):])
        p_out[1].append(s_new)
        p_out[2].append(mrow.reshape(bp, tp, 160))
        p_out[3].append(dk.reshape(bp, tp, 8, 32))
        p_out[4].append(dv.reshape(bp, tp, 4, 64))

        xs, (qkv, z, ba, mrow, kp, qlat, qrope, dq, dk, dv, dkb, dvb) = front(xs, cos_s, sin_s)
        ao, s_new = _gdn(qkv, z, ba, state_dn_conv, state_dn_S, l, cw, alog, dtb, gout,
                         nb=bs, t=ts)
        qlat_h = qlat.reshape(bs, ts, 4, 128).transpose(0, 2, 1, 3).reshape(bs, 4 * ts, 128)
        qr_h = qrope.reshape(bs, ts, 4, 32).transpose(0, 2, 1, 3).reshape(bs, 4 * ts, 32)
        bo, co = _paged(page_table, cache_mla_t, cache_dk_t, cache_dv_t, l, qlat_h, qr_h,
                        dq.reshape(bs, ts, 256).astype(F32), mrow.reshape(bs, ts, 160),
                        dk.reshape(bs, ts, 256), dv.reshape(bs, ts, 256), wuv_bd,
                        lq1, lk1, lq2, lk2, gsub4, nb=bs, ts=ts, lam_init=lam_init)
        xs = back(xs, ao, bo.reshape(bs * ts, 256), co.reshape(bs * ts, 256))
        s_out[0].append(qkv.reshape(bs, ts, 1536)[:, ts - (CONV_W - 1):])
        s_out[1].append(s_new)
        s_out[2].append(mrow.reshape(bs, ts, 160))
        s_out[3].append(dk.reshape(bs, ts, 8, 32))
        s_out[4].append(dv.reshape(bs, ts, 4, 64))

    p_conv, p_s, p_mla, p_dk, p_dv = [jnp.stack(a, axis=0) for a in p_out]
    s_conv, s_s, s_mla, s_dk, s_dv = [jnp.stack(a, axis=0) for a in s_out]
    return (xp.reshape(bp, tp, d), xs.reshape(bs, ts, d), p_mla, p_dk, p_dv, p_s, p_conv,
            s_mla, s_dk, s_dv, s_s, s_conv)
```

```python
import functools
import math

import jax
import jax.numpy as jnp
import numpy as np
from jax import lax
from jax.experimental import pallas as pl
from jax.experimental.pallas import tpu as pltpu

F32 = jnp.float32
BF16 = jnp.bfloat16
EPS = 1e-6
ROPE_THETA = 10000.0
CONV_W = 4
DN_CHUNK = 64
PAGE = 128
LANES = 128
LOG2E = math.log2(math.e)
MXU_N = 256
VMEM_LIMIT = 52 * 1024 * 1024
NEG = -1e30

_P_QKV, _P_Z, _P_BA, _P_CQ, _P_KR, _P_DQ, _P_DK, _P_DV, _P_END = (
    0, 1536, 2048, 2304, 2560, 2816, 3072, 3328, 3840)


def _cp(sem):
    return pltpu.CompilerParams(dimension_semantics=sem, vmem_limit_bytes=VMEM_LIMIT)


def _tile(m, pref):
    t = pref
    while m % t:
        t //= 2
    return t


def _mm(a, b):
    return jnp.dot(a.astype(BF16), b.astype(BF16), preferred_element_type=F32)


def _mm_nt(a, b):
    return lax.dot_general(a.astype(BF16), b.astype(BF16), (((1,), (1,)), ((), ())),
                           preferred_element_type=F32)


def _mm_tn(a, b):
    return lax.dot_general(a.astype(BF16), b.astype(BF16), (((0,), (0,)), ((), ())),
                           preferred_element_type=F32)


def _rms(x, g, n=None):
    n = x.shape[-1] if n is None else n
    ms = jnp.sum(x * x, axis=-1, keepdims=True) * (1.0 / n)
    return x * lax.rsqrt(ms + EPS) * g


def _silu(x):
    return x / (1.0 + jnp.exp(-x))


def _lane_group(lane, width, idx):
    return (lane >= idx * width) & (lane < (idx + 1) * width)


def _rep(x, k):
    return x if k == 1 else jnp.concatenate([x] * k, axis=1)


def _ffn_body(x, gpre_ref, wg_ref, wu_ref, wd_ref, gpost_ref, act_scr):
    h = _rms(x, gpre_ref[...]).astype(BF16)
    ff = wg_ref.shape[1]
    cw = MXU_N
    for c in range(ff // cw):
        a = jnp.dot(h, wg_ref[:, c * cw:(c + 1) * cw], preferred_element_type=F32)
        b = jnp.dot(h, wu_ref[:, c * cw:(c + 1) * cw], preferred_element_type=F32)
        act_scr[:, c * cw:(c + 1) * cw] = (_silu(a) * b).astype(BF16)
    y = jnp.dot(act_scr[...], wd_ref[...], preferred_element_type=F32)
    return x + 0.5 * _rms(y, gpost_ref[...])


def _ffn_kernel(x_ref, gpre_ref, wg_ref, wu_ref, wd_ref, gpost_ref, o_ref, act_scr):
    o_ref[...] = _ffn_body(x_ref[...], gpre_ref, wg_ref, wu_ref, wd_ref, gpost_ref, act_scr)


def _outproj_ffn_kernel(x_ref, ao_ref, bo_ref, co_ref, wo_ref, gmix_ref, gpre_ref, wg_ref, wu_ref,
                        wd_ref, gpost_ref, o_ref, act_scr):
    y = (_mm(ao_ref[...], wo_ref[0:512, :]) + _mm(bo_ref[...], wo_ref[512:768, :])
         + _mm(co_ref[...], wo_ref[768:1024, :]))
    x = x_ref[...] + _rms(y, gmix_ref[...])
    o_ref[...] = _ffn_body(x, gpre_ref, wg_ref, wu_ref, wd_ref, gpost_ref, act_scr)


def _resident(a):
    return pl.BlockSpec(a.shape, lambda i: (0,) * a.ndim, pipeline_mode=pl.Buffered(1))


def _ffn(x, gpre, wg, wu, wd, gpost, mix=None):
    m, d = x.shape
    ff = wg.shape[1]
    assert ff % MXU_N == 0
    tm = _tile(m, 512)
    row = lambda n: pl.BlockSpec((tm, n), lambda i: (i, 0))
    weights = [gpre, wg, wu, wd, gpost]
    if mix is None:
        kern, ins, in_specs = _ffn_kernel, [x], [row(d)]
    else:
        ao, bo, co, wo, gmix = mix
        kern, ins = _outproj_ffn_kernel, [x, ao, bo, co, wo, gmix]
        in_specs = [row(d), row(512), row(256), row(256), _resident(wo), _resident(gmix)]
    return pl.pallas_call(
        kern,
        out_shape=jax.ShapeDtypeStruct((m, d), F32),
        grid=(m // tm,),
        in_specs=in_specs + [_resident(w) for w in weights],
        out_specs=row(d),
        scratch_shapes=[pltpu.VMEM((tm, ff), BF16)],
        compiler_params=_cp(("parallel",)),
        name="ffn" if mix is None else "outproj_ffn",
    )(*ins, *weights)


def _proj_kernel(x_ref, g_ref, w_ref, cos_ref, sin_ref, gq_ref, wuq_ref, wuk_ref, gkv_ref,
                 *refs, mla_qscale, diff_qscale, feature_major, n_alias):
    qkv_ref, z_ref, ba_ref, kp_ref, qlat_ref, qrope_ref, dq_ref, dvb_ref = refs[n_alias:n_alias + 8]
    extra = refs[n_alias + 8:]
    h = _rms(x_ref[...], g_ref[...]).astype(BF16)

    def proj(lo, hi):
        return jnp.dot(h, w_ref[:, lo:hi], preferred_element_type=F32)

    qkv_ref[...] = proj(_P_QKV, _P_Z)
    z_ref[...] = proj(_P_Z, _P_BA)
    ba_ckv = proj(_P_BA, _P_CQ)
    ba_ref[...] = ba_ckv[:, 0:128]
    cos = cos_ref[...]
    sin = sin_ref[...]
    cq = _rms(proj(_P_CQ, _P_KR), gq_ref[...], n=192)
    qb = _mm(cq, wuq_ref[...])
    qlat_ref[...] = (_mm(qb[:, 0:256], wuk_ref[...]) * mla_qscale).astype(BF16)
    qrope_ref[...] = ((qb[:, 256:384] * cos + qb[:, 384:512] * sin) * mla_qscale).astype(BF16)
    ckv = _rms(ba_ckv[:, 128:256], gkv_ref[...])
    kr2 = proj(_P_KR, _P_DQ)
    kr = kr2[:, 0:128] * cos + kr2[:, 128:256] * sin
    kp_ref[:, 0:128] = ckv.astype(BF16)
    kp_ref[:, 128:256] = kr.astype(BF16)
    dq_ref[...] = (proj(_P_DQ, _P_DK) * diff_qscale).astype(BF16)
    dk = proj(_P_DK, _P_DV)
    dvd = proj(_P_DV, _P_END)
    dvb_ref[...] = dvd.astype(BF16)
    lane = lax.broadcasted_iota(jnp.int32, (dvd.shape[0], LANES), 1)
    dv = [jnp.where(lane < 64, dvd[:, (2 * j) * 128:(2 * j + 1) * 128],
                    dvd[:, (2 * j + 1) * 128:(2 * j + 2) * 128]) for j in range(2)]
    if feature_major:
        kpt_ref, dkbt_ref, rowt_ref, dkt_ref, dvt_ref = extra
        ckv_t, kr_t, dk_t = ckv.T, kr.T, dk.T
        kpt_ref[0:128, :] = ckv_t.astype(BF16)
        kpt_ref[128:256, :] = kr_t.astype(BF16)
        dkbt_ref[...] = dk_t.astype(BF16)
        rowt_ref[0:128, :] = ckv_t
        rowt_ref[128:160, :] = kr_t[0:32, :]
        dkt_ref[...] = dk_t
        for j in range(2):
            dvt_ref[j * 128:(j + 1) * 128, :] = dv[j].T
    else:
        row_ref, dk_ref, dv_ref = extra
        row_ref[:, 0:128] = ckv
        row_ref[:, 128:160] = kr[:, 0:32]
        dk_ref[...] = dk
        for j in range(2):
            dv_ref[:, j * 128:(j + 1) * 128] = dv[j]


def _proj(x, g, w, cos, sin, gq, wuq, wuk, gkv, *, mla_qscale, diff_qscale, seq=None):
    m, d = x.shape
    feature_major = seq is not None
    tm = _tile(seq[3] if feature_major else m, 512)
    row = lambda n: pl.BlockSpec((tm, n), lambda i: (i, 0))
    full = _resident
    outs = [(1536, F32), (512, F32), (128, F32), (256, BF16), (512, BF16), (128, BF16),
            (256, BF16), (512, BF16)]
    out_shape = [jax.ShapeDtypeStruct((m, n), dt) for n, dt in outs]
    out_specs = [row(n) for n, _ in outs]
    ins = [x, g, w, cos, sin, gq, wuq, wuk, gkv]
    in_specs = [row(d), full(g), full(w), row(128), row(128), full(gq), full(wuq),
                full(wuk), full(gkv)]
    aliases = {}
    n_alias = 0
    if feature_major:
        layer, depth, nb, t, stacked = seq
        nt = t // tm
        out_shape += [jax.ShapeDtypeStruct((nb, 256, t), BF16)] * 2
        out_specs += [pl.BlockSpec((None, 256, tm), lambda i: (i // nt, 0, i % nt))] * 2
        for n in (160, 256, 256):
            out_shape.append(jax.ShapeDtypeStruct((depth, nb, n, t), F32))
            out_specs.append(pl.BlockSpec((None, None, n, tm),
                                          lambda i: (layer, i // nt, 0, i % nt)))
        n_alias = len(stacked)
        aliases = {len(ins) + k: len(outs) + 2 + k for k in range(n_alias)}
        ins += list(stacked)
        in_specs += [pl.BlockSpec(memory_space=pl.ANY)] * n_alias
    else:
        for n in (160, 256, 256):
            out_shape.append(jax.ShapeDtypeStruct((m, n), F32))
            out_specs.append(row(n))
    kern = functools.partial(_proj_kernel, mla_qscale=mla_qscale, diff_qscale=diff_qscale,
                             feature_major=feature_major, n_alias=n_alias)
    return pl.pallas_call(
        kern,
        out_shape=out_shape,
        grid=(m // tm,),
        in_specs=in_specs,
        out_specs=out_specs,
        input_output_aliases=aliases,
        compiler_params=_cp(("parallel",)),
        name="proj_in",
    )(*ins)


def _gdn_kernel(x_ref, z_ref, ba_ref, conv0_ref, s0_ref, cw_ref, alog_ref, dtb_ref, gout_ref,
                *refs, nbb, t_in, t_cmp, t_valid):
    o_ref, sout_ref, xs_scr, s_scr = refs[-4:]
    _gdn_body(x_ref, z_ref, ba_ref, conv0_ref, s0_ref, cw_ref, alog_ref, dtb_ref, gout_ref,
              o_ref, sout_ref, xs_scr, s_scr, nbb=nbb, t_in=t_in, t_cmp=t_cmp, t_valid=t_valid)


def _gdn_body(x_ref, z_ref, ba_ref, conv0_ref, s0_ref, cw_ref, alog_ref, dtb_ref, gout_ref,
              o_ref, sout_ref, xs_scr, s_scr, *, nbb, t_in, t_cmp, t_valid):
    C = DN_CHUNK
    nc = t_cmp // C
    tb = pl.program_id(1)

    @pl.when(tb == 0)
    def _():
        s_scr[...] = s0_ref[...]
        for bb in range(nbb):
            xs_scr[bb, 5:8, :] = conv0_ref[bb]

    @pl.when(tb != 0)
    def _():
        for bb in range(nbb):
            xs_scr[bb, 5:8, :] = xs_scr[bb, 5 + t_in:8 + t_in, :]

    for bb in range(nbb):
        xs_scr[bb, 8:8 + t_in, :] = x_ref[bb * t_in:(bb + 1) * t_in, :]
        if t_cmp > t_in:
            xs_scr[bb, 8 + t_in:8 + t_cmp, :] = jnp.zeros((t_cmp - t_in, xs_scr.shape[2]), F32)

    ri = lax.broadcasted_iota(jnp.int32, (C, C), 0)
    ci = lax.broadcasted_iota(jnp.int32, (C, C), 1)
    incl = ri >= ci
    strict = ri > ci
    tril = jnp.where(incl, 1.0, 0.0).astype(F32)
    alog = alog_ref[...]
    dtb = dtb_ref[...]
    gout = gout_ref[...]

    trip = []
    for bb in range(nbb):
        for c in range(nc):
            r0 = c * C
            y = xs_scr[bb, 5 + r0:5 + r0 + C, :] * cw_ref[0:1, :]
            for i in range(1, CONV_W):
                y = y + xs_scr[bb, 5 + r0 + i:5 + r0 + i + C, :] * cw_ref[i:i + 1, :]
            qkv = _silu(y)
            if t_in >= t_cmp:
                ba = ba_ref[bb * t_in + r0:bb * t_in + r0 + C, :]
            else:
                ba = jnp.concatenate([ba_ref[bb * t_in:(bb + 1) * t_in, :],
                                      jnp.zeros((t_cmp - t_in, LANES), F32)], axis=0)
            bet_all = 1.0 / (1.0 + jnp.exp(-ba))
            xa = ba + dtb
            g_all = -jnp.exp(alog) * (jnp.maximum(xa, 0.0)
                                      + jnp.log(1.0 + jnp.exp(-jnp.abs(xa))))
            if t_valid < t_cmp:
                valid = lax.broadcasted_iota(jnp.int32, (C, 1), 0) + r0 < t_valid
                qkv = jnp.where(valid, qkv, 0.0)
                bet_all = jnp.where(valid, bet_all, 0.0)
                g_all = jnp.where(valid, g_all, 0.0)
            gam_all = jnp.dot(tril, g_all, preferred_element_type=F32,
                              precision=lax.Precision.HIGHEST)
            gam_t = gam_all.T
            for h in range(4):
                q = qkv[:, h * 128:(h + 1) * 128]
                k = qkv[:, 512 + h * 128:512 + (h + 1) * 128]
                v = qkv[:, 1024 + h * 128:1024 + (h + 1) * 128]
                q = q * lax.rsqrt(jnp.sum(q * q, axis=-1, keepdims=True) + EPS) * (128.0 ** -0.5)
                k = k * lax.rsqrt(jnp.sum(k * k, axis=-1, keepdims=True) + EPS)
                bet = bet_all[:, h:h + 1]
                gcol = gam_all[:, 4 + h:5 + h]
                grow = gam_t[4 + h:5 + h, :]
                glast = gam_all[C - 1:C, 4 + h:5 + h]
                diff = gcol - grow
                decay = jnp.where(incl, jnp.exp(jnp.where(incl, diff, 0.0)), 0.0)
                eg = jnp.exp(gcol)
                trip.append(dict(bb=bb, c=c, h=h, q=q, k=k, v=v, bet=bet, decay=decay, eg=eg,
                                 kd=k * jnp.exp(glast - gcol), sdec=jnp.exp(glast)))

    for t in trip:
        t["kk"] = _mm_nt(t["k"], t["k"])
    for t in trip:
        t["qk"] = _mm_nt(t["q"], t["k"]) * t["decay"]
    for t in trip:
        t["tp"] = jnp.where(strict, t["bet"] * t["kk"] * t["decay"], 0.0)
        t["x"] = jnp.concatenate([t["v"] * t["bet"], t["k"] * (t["bet"] * t["eg"])], axis=-1)
    b16r, b16c = lax.shift_right_logical(ri, 4), lax.shift_right_logical(ci, 4)
    b32r, b32c = lax.shift_right_logical(ri, 5), lax.shift_right_logical(ci, 5)
    in16 = b16r == b16c
    in32 = b32r == b32c
    eye = jnp.where(ri == ci, 1.0, 0.0).astype(F32)
    for t in trip:
        t["td"] = jnp.where(in16, t["tp"], 0.0)
    for t in trip:
        t["s2"] = _mm(t["td"], t["td"])
    for t in trip:
        t["s4"] = _mm(t["s2"], t["s2"])
        t["a"] = eye - t["td"] + t["s2"] - _mm(t["td"], t["s2"])
    for t in trip:
        t["s8"] = _mm(t["s4"], t["s4"])
    for t in trip:
        t["bm"] = t["s4"] + t["s8"] + _mm(t["s4"], t["s8"])
    for t in trip:
        t["p"] = t["a"] + _mm(t["a"], t["bm"])
    for t in trip:
        t["lp"] = _mm(jnp.where(in32 & ~in16, t["tp"], 0.0), t["p"])
    for t in trip:
        t["p"] = t["p"] - _mm(t["p"], t["lp"])
    for t in trip:
        t["lp"] = _mm(jnp.where(in32, 0.0, t["tp"]), t["p"])
    for t in trip:
        t["p"] = t["p"] - _mm(t["p"], t["lp"])
    for t in trip:
        t["x"] = _mm(t["p"], t["x"])
    state = {(bb, h): s_scr[bb, h] for bb in range(nbb) for h in range(4)}
    for c in range(nc):
        cur = [t for t in trip if t["c"] == c]
        for t in cur:
            s_h = state[(t["bb"], t["h"])]
            t["ws"] = _mm(t["x"][:, 128:256], s_h)
            t["qs"] = _mm(t["q"], s_h)
        for t in cur:
            t["u"] = t["x"][:, 0:128] - t["ws"]
        for t in cur:
            t["o"] = t["eg"] * t["qs"] + _mm(t["qk"], t["u"])
            key = (t["bb"], t["h"])
            state[key] = t["sdec"] * state[key] + _mm_tn(t["kd"], t["u"])
        r0 = c * C
        if r0 < t_in:
            n = min(C, t_in - r0)
            for t in cur:
                bb, h = t["bb"], t["h"]
                zz = z_ref[bb * t_in + r0:bb * t_in + r0 + n, h * 128:(h + 1) * 128]
                o_ref[bb * t_in + r0:bb * t_in + r0 + n, h * 128:(h + 1) * 128] = (
                    _rms(t["o"][0:n, :], gout) * _silu(zz)).astype(o_ref.dtype)
    for (bb, h), s_h in state.items():
        s_scr[bb, h] = s_h

    @pl.when(tb == pl.num_programs(1) - 1)
    def _():
        sout_ref[...] = s_scr[...]


def _gdn(qkv, z, ba, conv0, s0, layer, cw, alog, dtb, gout, s_stack, *, nb, t, out_layer):
    if t >= 128:
        nbb = 1
        t_in = t_cmp = t_valid = _tile(t, 256)
    else:
        nbb = _tile(nb, 4)
        t_in, t_cmp, t_valid = t, DN_CHUNK, t
    nt = t // t_in
    rows = lambda n: pl.BlockSpec((nbb * t_in, n), lambda b, j: (b * nt + j, 0))
    full = lambda a: pl.BlockSpec(a.shape, lambda b, j: (0,) * a.ndim)
    kern = functools.partial(_gdn_kernel, nbb=nbb, t_in=t_in, t_cmp=t_cmp, t_valid=t_valid)
    ins = [qkv, z, ba, conv0, s0, cw, alog, dtb, gout, s_stack]
    in_specs = [rows(1536), rows(512), rows(128),
                pl.BlockSpec((None, nbb, 3, 1536), lambda b, j: (layer, b, 0, 0)),
                pl.BlockSpec((None, nbb, 4, 128, 128), lambda b, j: (layer, b, 0, 0, 0)),
                full(cw), full(alog), full(dtb), full(gout), pl.BlockSpec(memory_space=pl.ANY)]
    return pl.pallas_call(
        kern,
        out_shape=[jax.ShapeDtypeStruct((nb * t, 512), BF16 if t_in % 16 == 0 else F32),
                   jax.ShapeDtypeStruct(s_stack.shape, F32)],
        grid=(nb // nbb, nt),
        in_specs=in_specs,
        out_specs=[rows(512), pl.BlockSpec((None, nbb, 4, 128, 128),
                                           lambda b, j: (out_layer, b, 0, 0, 0))],
        scratch_shapes=[pltpu.VMEM((nbb, 8 + t_cmp, 1536), F32),
                        pltpu.VMEM((nbb, 4, 128, 128), F32)],
        input_output_aliases={len(ins) - 1: 1},
        compiler_params=_cp(("parallel", "arbitrary")),
        name="gdn",
    )(*ins)


def _causal_sweep(i, tq, tk, nheads, step):
    n_full = (i * tq) // tk
    n_tot = ((i + 1) * tq + tk - 1) // tk

    def full_body(j, carry):
        step(j, None)
        return carry

    lax.fori_loop(0, n_full, full_body, 0)

    def diag_body(j, carry):
        qpos = i * tq + lax.broadcasted_iota(jnp.int32, (tq, tk), 0)
        kpos = j * tk + lax.broadcasted_iota(jnp.int32, (tq, tk), 1)
        mask = kpos <= qpos
        step(j, jnp.concatenate([mask] * nheads, axis=0))
        return carry

    lax.fori_loop(n_full, n_tot, diag_body, 0)


def _online_softmax(s, m_scr, tk):
    m_prev = m_scr[...]
    m_next = jnp.maximum(m_prev, jnp.max(s, axis=-1, keepdims=True))
    p = jnp.exp2(s - _rep(m_next, tk // LANES)).astype(BF16)
    alpha = jnp.exp2(m_prev - m_next)
    m_scr[...] = m_next
    return p, alpha


def _mla_attn_kernel(qlat_ref, qrope_ref, kp_ref, kpt_ref, wuv_ref, o_ref, q_scr, m_scr, acc_scr,
                     *, tq, tk):
    i = pl.program_id(1)
    qr = qrope_ref[...]
    lane = lax.broadcasted_iota(jnp.int32, qr.shape, 1)
    for h in range(4):
        q_scr[h * tq:(h + 1) * tq, 0:128] = qlat_ref[:, h * 128:(h + 1) * 128]
        q_scr[h * tq:(h + 1) * tq, 128:256] = jnp.where(_lane_group(lane, 32, h), qr,
                                                        jnp.zeros_like(qr))
    m_scr[...] = jnp.full(m_scr.shape, -jnp.inf, F32)
    acc_scr[...] = jnp.zeros(acc_scr.shape, F32)
    ones = jnp.ones((tk, LANES), BF16)

    def step(j, mask):
        start = pl.multiple_of(j * tk, tk)
        s = jnp.dot(q_scr[...], kpt_ref[:, pl.ds(start, tk)], preferred_element_type=F32)
        if mask is not None:
            s = jnp.where(mask, s, NEG)
        p, alpha = _online_softmax(s, m_scr, tk)
        vext = jnp.concatenate([kp_ref[pl.ds(start, tk), 0:128], ones], axis=1)
        acc_scr[...] = acc_scr[...] * _rep(alpha, 2) + jnp.dot(
            p, vext, preferred_element_type=F32)

    _causal_sweep(i, tq, tk, 4, step)
    o = (acc_scr[:, 0:128] / acc_scr[:, 128:256]).astype(BF16)
    bo = jnp.dot(o[0:tq], wuv_ref[0:128, :], preferred_element_type=F32)
    for h in range(1, 4):
        bo = bo + jnp.dot(o[h * tq:(h + 1) * tq], wuv_ref[h * 128:(h + 1) * 128, :],
                          preferred_element_type=F32)
    o_ref[...] = bo.astype(BF16)


def _mla_attn(qlat, qrope, kp, kpt, wuv, *, nb, t):
    tq = _tile(t, 512)
    tk = _tile(t, 512)
    nq = t // tq
    kern = functools.partial(_mla_attn_kernel, tq=tq, tk=tk)
    return pl.pallas_call(
        kern,
        out_shape=jax.ShapeDtypeStruct((nb * t, 256), BF16),
        grid=(nb, nq),
        in_specs=[pl.BlockSpec((tq, 512), lambda b, i: (b * nq + i, 0)),
                  pl.BlockSpec((tq, 128), lambda b, i: (b * nq + i, 0)),
                  pl.BlockSpec((t, 256), lambda b, i: (b, 0)),
                  pl.BlockSpec((None, 256, t), lambda b, i: (b, 0, 0)),
                  pl.BlockSpec(wuv.shape, lambda b, i: (0, 0))],
        out_specs=pl.BlockSpec((tq, 256), lambda b, i: (b * nq + i, 0)),
        scratch_shapes=[pltpu.VMEM((4 * tq, 256), BF16), pltpu.VMEM((4 * tq, LANES), F32),
                        pltpu.VMEM((4 * tq, 256), F32)],
        compiler_params=_cp(("parallel", "parallel")),
        name="mla_attn",
    )(qlat, qrope, kp, kpt, wuv)


def _diff_lambda(lq1, lk1, lq2, lk2, lam_init):
    return (jnp.exp(jnp.sum(lq1 * lk1, axis=-1, keepdims=True))
            - jnp.exp(jnp.sum(lq2 * lk2, axis=-1, keepdims=True)) + lam_init)


def _diff_attn_kernel(q_ref, kt_ref, v_ref, lq1_ref, lk1_ref, lq2_ref, lk2_ref, gsub_ref, o_ref,
                      q_scr, m_scr, acc_scr, *, tq, tk, lam_init):
    i = pl.program_id(1)
    q = q_ref[...]
    lane = lax.broadcasted_iota(jnp.int32, q.shape, 1)
    for h in range(8):
        q_scr[h * tq:(h + 1) * tq, :] = jnp.where(_lane_group(lane, 32, h), q, jnp.zeros_like(q))
    m_scr[...] = jnp.full(m_scr.shape, -jnp.inf, F32)
    acc_scr[...] = jnp.zeros(acc_scr.shape, F32)
    ones = jnp.ones((tk, LANES), BF16)

    def step(j, mask):
        start = pl.multiple_of(j * tk, tk)
        s = jnp.dot(q_scr[...], kt_ref[:, pl.ds(start, tk)], preferred_element_type=F32)
        if mask is not None:
            s = jnp.where(mask, s, NEG)
        p, alpha = _online_softmax(s, m_scr, tk)
        alpha = _rep(alpha, 2)
        for c in range(4):
            vext = jnp.concatenate([v_ref[pl.ds(start, tk), c * 128:(c + 1) * 128], ones], axis=1)
            r = slice(2 * c * tq, (2 * c + 2) * tq)
            acc_scr[r, :] = acc_scr[r, :] * alpha[r, :] + jnp.dot(
                p[r, :], vext, preferred_element_type=F32)

    _causal_sweep(i, tq, tk, 8, step)
    lam = _diff_lambda(lq1_ref[...], lk1_ref[...], lq2_ref[...], lk2_ref[...], lam_init)
    on = acc_scr[:, 0:128] / acc_scr[:, 128:256]
    lane128 = lax.broadcasted_iota(jnp.int32, (tq, LANES), 1)
    outs = []
    for c in range(4):
        d = on[2 * c * tq:(2 * c + 1) * tq] - lam * on[(2 * c + 1) * tq:(2 * c + 2) * tq]
        outs.append(_rms(d, gsub_ref[...]) * (1.0 - lam_init))
    for j in range(2):
        o_ref[:, j * 128:(j + 1) * 128] = jnp.where(
            lane128 < 64, outs[2 * j], outs[2 * j + 1]).astype(BF16)


def _diff_attn(q, kt, vdup, lq1, lk1, lq2, lk2, gsub, *, nb, t, lam_init):
    tq = _tile(t, 256)
    tk = _tile(t, 512)
    nq = t // tq
    kern = functools.partial(_diff_attn_kernel, tq=tq, tk=tk, lam_init=lam_init)
    vec = lambda a: pl.BlockSpec(a.shape, lambda b, i: (0, 0))
    return pl.pallas_call(
        kern,
        out_shape=jax.ShapeDtypeStruct((nb * t, 256), BF16),
        grid=(nb, nq),
        in_specs=[pl.BlockSpec((tq, 256), lambda b, i: (b * nq + i, 0)),
                  pl.BlockSpec((None, 256, t), lambda b, i: (b, 0, 0)),
                  pl.BlockSpec((t, 512), lambda b, i: (b, 0)),
                  vec(lq1), vec(lk1), vec(lq2), vec(lk2), vec(gsub)],
        out_specs=pl.BlockSpec((tq, 256), lambda b, i: (b * nq + i, 0)),
        scratch_shapes=[pltpu.VMEM((8 * tq, 256), BF16), pltpu.VMEM((8 * tq, LANES), F32),
                        pltpu.VMEM((8 * tq, 256), F32)],
        compiler_params=_cp(("parallel", "parallel")),
        name="diff_attn",
    )(q, kt, vdup, lq1, lk1, lq2, lk2, gsub)


def _paged_kernel(pt_ref, mla_hbm, dk_hbm, dv_hbm, qlat_ref, qr_ref, dq_ref, row_ref, dkn_ref,
                  dvn_ref, wuv_ref, lq1_ref, lk1_ref, lq2_ref, lk2_ref, gsub_ref, bo_ref, co_ref,
                  mla_buf, dk_buf, dv_buf, sem, dqb_scr, ma_scr, la_scr, aa_scr, md_scr, ld_scr,
                  ad_scr, *, g, nslot, ts, layer, lam_init):
    b = pl.program_id(0)
    nb = pl.num_programs(0)

    def copies(seq, slot):
        out = []
        for gi in range(g):
            pg = pt_ref[seq, slot * g + gi]
            for hbm, buf in ((mla_hbm, mla_buf), (dk_hbm, dk_buf), (dv_hbm, dv_buf)):
                out.append(pltpu.make_async_copy(hbm.at[layer, pg], buf.at[slot, gi],
                                                 sem.at[slot]))
        return out

    last = b == nb - 1
    nxt_b = jnp.where(last, b, b + 1)

    @pl.when(b == 0)
    def _():
        for k in range(nslot):
            for c in copies(b, k):
                c.start()

    q = dq_ref[...]
    lane = lax.broadcasted_iota(jnp.int32, q.shape, 1)
    for h in range(8):
        dqb_scr[h * ts:(h + 1) * ts, :] = jnp.where(_lane_group(lane, 32, h), q, 0.0)
    ma_scr[...] = jnp.full(ma_scr.shape, -jnp.inf, F32)
    la_scr[...] = jnp.zeros(la_scr.shape, F32)
    aa_scr[...] = jnp.zeros(aa_scr.shape, F32)
    md_scr[...] = jnp.full(md_scr.shape, -jnp.inf, F32)
    ld_scr[...] = jnp.zeros(ld_scr.shape, F32)
    ad_scr[...] = jnp.zeros(ad_scr.shape, F32)

    qlat = qlat_ref[...].astype(BF16)
    qr = qr_ref[...].astype(BF16)
    dqb = dqb_scr[...].astype(BF16)

    def update(s_list, pv, m_scr, l_scr, acc_scr):
        s = jnp.concatenate(s_list, axis=-1) if len(s_list) > 1 else s_list[0]
        m_old = m_scr[...]
        m_new = jnp.maximum(m_old, jnp.max(s, axis=-1, keepdims=True))
        p = jnp.exp2(s - m_new)
        corr = jnp.exp2(m_old - m_new)
        l_scr[...] = l_scr[...] * corr + jnp.sum(p, axis=-1, keepdims=True)
        pb = p.astype(BF16)
        acc = acc_scr[...] * corr
        off = 0
        for idx, sb in enumerate(s_list):
            n = sb.shape[1]
            acc = acc + pv(pb[:, off:off + n], idx)
            off += n
        acc_scr[...] = acc
        m_scr[...] = m_new

    def consume(slot):
        kts = [mla_buf[slot, gi].astype(BF16) for gi in range(g)]
        sa = [jnp.dot(qlat, kt[0:128, :], preferred_element_type=F32)
              + jnp.dot(qr, kt[128:160, :], preferred_element_type=F32) for kt in kts]
        update(sa, lambda pb, idx: _mm_nt(pb, kts[idx][0:128, :]), ma_scr, la_scr, aa_scr)
        sd = [jnp.dot(dqb, dk_buf[slot, gi].astype(BF16), preferred_element_type=F32)
              for gi in range(g)]
        update(sd, lambda pb, idx: _mm_nt(pb, dv_buf[slot, idx]), md_scr, ld_scr, ad_scr)

    for k in range(nslot):
        for c in copies(b, k):
            c.wait()
        consume(k)
        for c in copies(nxt_b, k):
            c.start()

    @pl.when(last)
    def _():
        for k in range(nslot):
            for c in copies(nxt_b, k):
                c.wait()

    def finalize():
        tsp = -(-ts // 16) * 16
        qpos = lax.broadcasted_iota(jnp.int32, (ts, tsp), 0)
        kpos = lax.broadcasted_iota(jnp.int32, (ts, tsp), 1)
        mask = kpos <= qpos

        def padded(ref):
            rows = ref[...]
            return jnp.concatenate([rows, jnp.zeros((tsp - ts, rows.shape[1]), F32)],
                                   axis=0).astype(BF16)

        rows = padded(row_ref)
        s = _mm_nt(qlat, rows[:, 0:128]) + _mm_nt(qr, rows[:, 128:160])
        update([jnp.where(jnp.concatenate([mask] * 4, axis=0), s, NEG)],
               lambda pb, idx: jnp.dot(pb, rows[:, 0:128], preferred_element_type=F32),
               ma_scr, la_scr, aa_scr)
        s = _mm_nt(dqb, padded(dkn_ref))
        vn = padded(dvn_ref)
        update([jnp.where(jnp.concatenate([mask] * 8, axis=0), s, NEG)],
               lambda pb, idx: jnp.dot(pb, vn, preferred_element_type=F32),
               md_scr, ld_scr, ad_scr)
        o = (aa_scr[...] / la_scr[...]).astype(BF16)
        bo = jnp.zeros((ts, 256), F32)
        for h in range(4):
            full = jnp.dot(o, wuv_ref[h * 128:(h + 1) * 128, :], preferred_element_type=F32)
            bo = bo + full[h * ts:(h + 1) * ts]
        bo_ref[...] = bo
        on = ad_scr[...] / ld_scr[...]
        lam = _diff_lambda(lq1_ref[...], lk1_ref[...], lq2_ref[...], lk2_ref[...], lam_init)
        lane = lax.broadcasted_iota(jnp.int32, (ts, 256), 1)
        d = jnp.zeros((ts, 256), F32)
        for c in range(4):
            dc = on[2 * c * ts:(2 * c + 1) * ts] - lam * on[(2 * c + 1) * ts:(2 * c + 2) * ts]
            d = jnp.where(_lane_group(lane, 64, c), dc, d)
        ms = jnp.zeros((ts, 256), F32)
        for c in range(4):
            sel = _lane_group(lane, 64, c)
            ssq = jnp.sum(jnp.where(sel, d * d, 0.0), axis=-1, keepdims=True) * (1.0 / 64)
            ms = jnp.where(sel, ssq, ms)
        co_ref[...] = d * lax.rsqrt(ms + EPS) * gsub_ref[...] * (1.0 - lam_init)

    finalize()


def _paged(page_table, cache_mla_t, cache_dk_t, cache_dv_t, layer, qlat, qr, dq, row, dkn, dvn,
           wuv, lq1, lk1, lq2, lk2, gsub4, *, nb, ts, lam_init):
    n_pages = page_table.shape[1]
    g = _tile(n_pages, 16)
    nslot = n_pages // g
    kern = functools.partial(_paged_kernel, g=g, nslot=nslot, ts=ts, layer=layer,
                             lam_init=lam_init)
    per_b = lambda r, n: pl.BlockSpec((None, r, n), lambda b, pt: (b, 0, 0))
    vec = lambda a: pl.BlockSpec(a.shape, lambda b, pt: (0, 0))
    hbm = pl.BlockSpec(memory_space=pl.ANY)
    in_specs = [hbm, hbm, hbm,
                per_b(4 * ts, 128), per_b(4 * ts, 32), per_b(ts, 256), per_b(ts, 160),
                per_b(ts, 256), per_b(ts, 256), vec(wuv), vec(lq1), vec(lk1), vec(lq2),
                vec(lk2), vec(gsub4)]
    grid_spec = pltpu.PrefetchScalarGridSpec(
        num_scalar_prefetch=1,
        grid=(nb,),
        in_specs=in_specs,
        out_specs=[per_b(ts, 256), per_b(ts, 256)],
        scratch_shapes=[pltpu.VMEM((nslot, g, 160, PAGE), F32),
                        pltpu.VMEM((nslot, g, 256, PAGE), F32),
                        pltpu.VMEM((nslot, g, 256, PAGE), F32),
                        pltpu.SemaphoreType.DMA((nslot,)),
                        pltpu.VMEM((8 * ts, 256), F32),
                        pltpu.VMEM((4 * ts, 1), F32), pltpu.VMEM((4 * ts, 1), F32),
                        pltpu.VMEM((4 * ts, 128), F32),
                        pltpu.VMEM((8 * ts, 1), F32), pltpu.VMEM((8 * ts, 1), F32),
                        pltpu.VMEM((8 * ts, 256), F32)])
    return pl.pallas_call(
        kern,
        out_shape=[jax.ShapeDtypeStruct((nb, ts, 256), F32),
                   jax.ShapeDtypeStruct((nb, ts, 256), F32)],
        grid_spec=grid_spec,
        compiler_params=_cp(("arbitrary",)),
        name="paged_attn",
    )(page_table, cache_mla_t, cache_dk_t, cache_dv_t,
      qlat, qr, dq, row, dkn, dvn, wuv, lq1, lk1, lq2, lk2, gsub4)


def _pack_w_in(w):
    d = w.shape[0]
    a_qkv, a_z, a_b, a_a, b_cq, b_ckv, b_kr, c_q, c_k, c_v = jnp.split(
        w, np.cumsum([1536, 512, 4, 4, 192, 128, 32, 256, 256]).tolist(), axis=1)
    z = lambda n: jnp.zeros((d, n), w.dtype)
    kr_sw = jnp.concatenate([-b_kr[:, 16:32], b_kr[:, 0:16]], axis=1)
    v4 = c_v.reshape(d, 4, 64)
    vdup = jnp.concatenate([v4, v4], axis=2).reshape(d, 512)
    return jnp.concatenate(
        [a_qkv, a_z, a_b, a_a, z(120), b_ckv, b_cq, z(64), jnp.tile(b_kr, (1, 4)),
         jnp.tile(kr_sw, (1, 4)), c_q, c_k, vdup], axis=1).astype(BF16)


def _pack_mla(w_uq, w_uk, w_uv):
    nope = w_uq[:, :, 0:64].reshape(192, 256)
    rope = w_uq[:, :, 64:96]
    rope_sw = jnp.concatenate([-rope[:, :, 16:32], rope[:, :, 0:16]], axis=2)
    wuq = jnp.concatenate([nope, rope.reshape(192, 128), rope_sw.reshape(192, 128)], axis=1)
    wuq = jnp.concatenate([wuq, jnp.zeros((64, 512), wuq.dtype)], axis=0)
    eye = jnp.eye(4, dtype=w_uk.dtype)
    wuk_bd = jnp.einsum('rhn,hg->hngr', w_uk, eye).reshape(256, 512)
    wuv_bd = jnp.einsum('rhe,hg->hrge', w_uv, eye).reshape(512, 256)
    return wuq.astype(BF16), wuk_bd.astype(BF16), wuv_bd.astype(BF16)


def _rope_tables(pos):
    freqs = jnp.power(ROPE_THETA, -jnp.arange(16, dtype=F32) / 16)
    ang = pos[:, None] * freqs[None, :]
    return jnp.tile(jnp.cos(ang), (1, 8)), jnp.tile(jnp.sin(ang), (1, 8))


def _lane_pad(v, n=LANES, offset=0):
    v = v.reshape(1, -1).astype(F32)
    return jnp.pad(v, ((0, 0), (offset, n - offset - v.shape[1])))


def kernel(x_prompt, x_sample, cache_mla, cache_diff_k, cache_diff_v, state_dn_S, state_dn_conv, page_table, g_f1_pre, g_f1_post, w_f1_gate, w_f1_up, w_f1_down, g_mix_pre, g_mix_post, w_in, w_out, dn_conv_w, dn_A_log, dn_dt_bias, dn_g_out, mla_g_q, mla_w_uq, mla_g_kv, mla_w_uk, mla_w_uv, diff_lam_q1, diff_lam_k1, diff_lam_q2, diff_lam_k2, diff_g_subln, g_f2_pre, g_f2_post, w_f2_gate, w_f2_up, w_f2_down):
    bp, tp, d = x_prompt.shape
    bs, ts, _ = x_sample.shape
    depth = w_in.shape[0]
    n_pages = page_table.shape[1]
    past_len = n_pages * PAGE
    mla_qscale = 96.0 ** -0.5 * LOG2E
    diff_qscale = 32.0 ** -0.5 * LOG2E

    cos_p, sin_p = _rope_tables(jnp.arange(tp, dtype=F32))
    cos_p, sin_p = jnp.tile(cos_p, (bp, 1)), jnp.tile(sin_p, (bp, 1))
    cos_s, sin_s = _rope_tables(past_len + jnp.arange(ts, dtype=F32))
    cos_s, sin_s = jnp.tile(cos_s, (bs, 1)), jnp.tile(sin_s, (bs, 1))

    n_pool = cache_mla.shape[1]
    cache_mla_t = jnp.swapaxes(cache_mla, 2, 3)
    cache_dk_t = jnp.transpose(cache_diff_k, (0, 1, 3, 4, 2)).reshape(depth, n_pool, 256, PAGE)
    cache_dv_t = jnp.transpose(cache_diff_v, (0, 1, 3, 4, 2)).reshape(depth, n_pool, 256, PAGE)
    zero_conv = jnp.zeros((1, bp, CONV_W - 1, 1536), F32)
    zero_s = jnp.zeros((1, bp, 4, 128, 128), F32)

    xp = x_prompt.reshape(bp * tp, d)
    xs = x_sample.reshape(bs * ts, d)
    p_conv = []
    p_fm = [jnp.zeros((depth, bp, n, tp), F32) for n in (160, 256, 256)]
    p_s = jnp.zeros((depth, bp, 4, 128, 128), F32)
    s_s = jnp.zeros((depth, bs, 4, 128, 128), F32)
    s_out = [[] for _ in range(4)]
    row1 = lambda v: v.reshape(1, -1).astype(F32)
    for l in range(depth):
        lam_init = 0.8 - 0.6 * math.exp(-0.3 * l)
        wg1, wu1, wd1 = (w_f1_gate[l].astype(BF16), w_f1_up[l].astype(BF16),
                         w_f1_down[l].astype(BF16))
        wg2, wu2, wd2 = (w_f2_gate[l].astype(BF16), w_f2_up[l].astype(BF16),
                         w_f2_down[l].astype(BF16))
        w_inp = _pack_w_in(w_in[l])
        wuq, wuk_bd, wuv_bd = _pack_mla(mla_w_uq[l], mla_w_uk[l], mla_w_uv[l])
        w_o = w_out[l].astype(BF16)
        gq = _lane_pad(mla_g_q[l], 256)
        gkv = row1(mla_g_kv[l])
        alog = _lane_pad(dn_A_log[l], offset=4)
        dtb = _lane_pad(dn_dt_bias[l], offset=4)
        gout = row1(dn_g_out[l])
        cw = dn_conv_w[l].astype(F32)
        lq1, lk1, lq2, lk2 = (_lane_pad(diff_lam_q1[l]), _lane_pad(diff_lam_k1[l]),
                              _lane_pad(diff_lam_q2[l]), _lane_pad(diff_lam_k2[l]))
        gsub2 = jnp.tile(row1(diff_g_subln[l]), (1, 2))
        gsub4 = jnp.tile(row1(diff_g_subln[l]), (1, 4))

        def front(x, cos, sin, seq=None):
            x = _ffn(x, row1(g_f1_pre[l]), wg1, wu1, wd1, row1(g_f1_post[l]))
            return x, _proj(x, row1(g_mix_pre[l]), w_inp, cos, sin, gq, wuq, wuk_bd, gkv,
                            mla_qscale=mla_qscale, diff_qscale=diff_qscale, seq=seq)

        def back(x, ao, bo, co):
            return _ffn(x, row1(g_f2_pre[l]), wg2, wu2, wd2, row1(g_f2_post[l]),
                        mix=(ao, bo, co, w_o, row1(g_mix_post[l])))

        xp, (qkv, z, ba, kp, qlat, qrope, dq, dvb, kpt, dkbt, *p_fm) = front(
            xp, cos_p, sin_p, seq=(l, depth, bp, tp, p_fm))
        ao, p_s = _gdn(qkv, z, ba, zero_conv, zero_s, 0, cw, alog, dtb, gout, p_s, nb=bp, t=tp,
                       out_layer=l)
        bo = _mla_attn(qlat, qrope, kp, kpt, wuv_bd, nb=bp, t=tp)
        co = _diff_attn(dq, dkbt, dvb, lq1, lk1, lq2, lk2, gsub2, nb=bp, t=tp, lam_init=lam_init)
        xp = back(xp, ao, bo, co)
        p_conv.append(qkv.reshape(bp, tp, 1536)[:, tp - (CONV_W - 1):])

        xs, (qkv, z, ba, kp, qlat, qrope, dq, dvb, mrow, dk, dv) = front(xs, cos_s, sin_s)
        ao, s_s = _gdn(qkv, z, ba, state_dn_conv, state_dn_S, l, cw, alog, dtb, gout, s_s,
                       nb=bs, t=ts, out_layer=l)
        qlat_h = qlat.reshape(bs, ts, 4, 128).transpose(0, 2, 1, 3).reshape(bs, 4 * ts, 128)
        qr_h = qrope.reshape(bs, ts, 4, 32).transpose(0, 2, 1, 3).reshape(bs, 4 * ts, 32)
        bo, co = _paged(page_table, cache_mla_t, cache_dk_t, cache_dv_t, l, qlat_h, qr_h,
                        dq.reshape(bs, ts, 256).astype(F32), mrow.reshape(bs, ts, 160),
                        dk.reshape(bs, ts, 256), dv.reshape(bs, ts, 256), wuv_bd,
                        lq1, lk1, lq2, lk2, gsub4, nb=bs, ts=ts, lam_init=lam_init)
        xs = back(xs, ao, bo.reshape(bs * ts, 256), co.reshape(bs * ts, 256))
        s_out[0].append(qkv.reshape(bs, ts, 1536)[:, ts - (CONV_W - 1):])
        s_out[1].append(mrow.reshape(bs, ts, 160))
        s_out[2].append(dk.reshape(bs, ts, 8, 32))
        s_out[3].append(dv.reshape(bs, ts, 4, 64))

    rowt, dkt, dvt = p_fm
    p_mla = jnp.swapaxes(rowt, 2, 3)
    p_dk = jnp.transpose(dkt.reshape(depth, bp, 8, 32, tp), (0, 1, 4, 2, 3))
    p_dv = jnp.transpose(dvt.reshape(depth, bp, 4, 64, tp), (0, 1, 4, 2, 3))
    s_conv, s_mla, s_dk, s_dv = [jnp.stack(a, axis=0) for a in s_out]
    return (xp.reshape(bp, tp, d), xs.reshape(bs, ts, d), p_mla, p_dk, p_dv, p_s,
            jnp.stack(p_conv, axis=0), s_mla, s_dk, s_dv, s_s, s_conv)
```

```python
import functools
import math

import jax
import jax.numpy as jnp
import numpy as np
from jax import lax
from jax.experimental import pallas as pl
from jax.experimental.pallas import tpu as pltpu

F32 = jnp.float32
BF16 = jnp.bfloat16
EPS = 1e-6
ROPE_THETA = 10000.0
CONV_W = 4
DN_CHUNK = 64
PAGE = 128
LANES = 128
LOG2E = math.log2(math.e)
MXU_N = 256
VMEM_LIMIT = 52 * 1024 * 1024
NEG = -1e30

_P_QKV, _P_Z, _P_BA, _P_CQ, _P_KR, _P_DQ, _P_DK, _P_DV, _P_END = (
    0, 1536, 2048, 2304, 2560, 2816, 3072, 3328, 3840)


def _cp(sem):
    return pltpu.CompilerParams(dimension_semantics=sem, vmem_limit_bytes=VMEM_LIMIT)


def _tile(m, pref):
    t = pref
    while m % t:
        t //= 2
    return t


def _mm(a, b):
    return jnp.dot(a.astype(BF16), b.astype(BF16), preferred_element_type=F32)


def _mm_nt(a, b):
    return lax.dot_general(a.astype(BF16), b.astype(BF16), (((1,), (1,)), ((), ())),
                           preferred_element_type=F32)


def _mm_tn(a, b):
    return lax.dot_general(a.astype(BF16), b.astype(BF16), (((0,), (0,)), ((), ())),
                           preferred_element_type=F32)


def _rms(x, g, n=None):
    n = x.shape[-1] if n is None else n
    ms = jnp.sum(x * x, axis=-1, keepdims=True) * (1.0 / n)
    return x * lax.rsqrt(ms + EPS) * g


def _silu(x):
    return x / (1.0 + jnp.exp(-x))


def _lane_group(lane, width, idx):
    return (lane >= idx * width) & (lane < (idx + 1) * width)


def _rep(x, k):
    return x if k == 1 else jnp.concatenate([x] * k, axis=1)


def _ffn_body(x, gpre_ref, wg_ref, wu_ref, wd_ref, gpost_ref, act_scr):
    h = _rms(x, gpre_ref[...]).astype(BF16)
    ff = wg_ref.shape[1]
    cw = MXU_N
    for c in range(ff // cw):
        a = jnp.dot(h, wg_ref[:, c * cw:(c + 1) * cw], preferred_element_type=F32)
        b = jnp.dot(h, wu_ref[:, c * cw:(c + 1) * cw], preferred_element_type=F32)
        act_scr[:, c * cw:(c + 1) * cw] = (_silu(a) * b).astype(BF16)
    y = jnp.dot(act_scr[...], wd_ref[...], preferred_element_type=F32)
    return x + 0.5 * _rms(y, gpost_ref[...])


def _ffn_kernel(x_ref, gpre_ref, wg_ref, wu_ref, wd_ref, gpost_ref, o_ref, act_scr):
    o_ref[...] = _ffn_body(x_ref[...], gpre_ref, wg_ref, wu_ref, wd_ref, gpost_ref, act_scr)


def _outproj_ffn_kernel(x_ref, ao_ref, bo_ref, co_ref, wo_ref, gmix_ref, gpre_ref, wg_ref, wu_ref,
                        wd_ref, gpost_ref, o_ref, act_scr):
    y = (_mm(ao_ref[...], wo_ref[0:512, :]) + _mm(bo_ref[...], wo_ref[512:768, :])
         + _mm(co_ref[...], wo_ref[768:1024, :]))
    x = x_ref[...] + _rms(y, gmix_ref[...])
    o_ref[...] = _ffn_body(x, gpre_ref, wg_ref, wu_ref, wd_ref, gpost_ref, act_scr)


def _resident(a):
    return pl.BlockSpec(a.shape, lambda i: (0,) * a.ndim, pipeline_mode=pl.Buffered(1))


def _ffn(x, gpre, wg, wu, wd, gpost, mix=None):
    m, d = x.shape
    ff = wg.shape[1]
    assert ff % MXU_N == 0
    tm = _tile(m, 512)
    row = lambda n: pl.BlockSpec((tm, n), lambda i: (i, 0))
    weights = [gpre, wg, wu, wd, gpost]
    if mix is None:
        kern, ins, in_specs = _ffn_kernel, [x], [row(d)]
    else:
        ao, bo, co, wo, gmix = mix
        kern, ins = _outproj_ffn_kernel, [x, ao, bo, co, wo, gmix]
        in_specs = [row(d), row(512), row(256), row(256), _resident(wo), _resident(gmix)]
    return pl.pallas_call(
        kern,
        out_shape=jax.ShapeDtypeStruct((m, d), F32),
        grid=(m // tm,),
        in_specs=in_specs + [_resident(w) for w in weights],
        out_specs=row(d),
        scratch_shapes=[pltpu.VMEM((tm, ff), BF16)],
        compiler_params=_cp(("parallel",)),
        name="ffn" if mix is None else "outproj_ffn",
    )(*ins, *weights)


def _proj_kernel(x_ref, g_ref, w_ref, cos_ref, sin_ref, gq_ref, wuq_ref, wuk_ref, gkv_ref,
                 *refs, mla_qscale, diff_qscale, n_alias, nt):
    fresh = nt is not None
    if fresh:
        cw_ref = refs[0]
        refs = refs[1 + n_alias:]
    qkv_ref, z_ref, ba_ref, kp_ref, qlat_ref, qrope_ref, dq_ref, dvb_ref = refs[0:8]
    extra = refs[8:]
    h = _rms(x_ref[...], g_ref[...]).astype(BF16)

    def proj(lo, hi):
        return jnp.dot(h, w_ref[:, lo:hi], preferred_element_type=F32)

    if fresh:
        tail_ref, xs_scr = extra[5:]
        tm = x_ref.shape[0]

        @pl.when(lax.rem(pl.program_id(0), nt) == 0)
        def _():
            xs_scr[0:8, :] = jnp.zeros((8, xs_scr.shape[1]), F32)

        xs_scr[8:8 + tm, :] = proj(_P_QKV, _P_Z)
        rc = min(tm, 128)
        for r0 in range(0, tm, rc):
            parts = _conv_silu_norm(lambda i: xs_scr[5 + r0 + i:5 + r0 + i + rc, :], cw_ref)
            for j, part in enumerate(parts):
                qkv_ref[r0:r0 + rc, j * 128:(j + 1) * 128] = part
        tail = xs_scr[tm:tm + 8, :]
        tail_ref[...] = tail
        xs_scr[0:8, :] = tail
    else:
        qkv_ref[...] = proj(_P_QKV, _P_Z)
    z_ref[...] = proj(_P_Z, _P_BA)
    ba_ckv = proj(_P_BA, _P_CQ)
    ba_ref[...] = ba_ckv[:, 0:128]
    cos = cos_ref[...]
    sin = sin_ref[...]
    cq = _rms(proj(_P_CQ, _P_KR), gq_ref[...], n=192)
    qb = _mm(cq, wuq_ref[...])
    qlat_ref[...] = (_mm(qb[:, 0:256], wuk_ref[...]) * mla_qscale).astype(BF16)
    qrope_ref[...] = ((qb[:, 256:384] * cos + qb[:, 384:512] * sin) * mla_qscale).astype(BF16)
    ckv = _rms(ba_ckv[:, 128:256], gkv_ref[...])
    kr2 = proj(_P_KR, _P_DQ)
    kr = kr2[:, 0:128] * cos + kr2[:, 128:256] * sin
    kp_ref[:, 0:128] = ckv.astype(BF16)
    kp_ref[:, 128:256] = kr.astype(BF16)
    dq_ref[...] = (proj(_P_DQ, _P_DK) * diff_qscale).astype(BF16)
    dk = proj(_P_DK, _P_DV)
    dvd = proj(_P_DV, _P_END)
    dvb_ref[...] = dvd.astype(BF16)
    lane = lax.broadcasted_iota(jnp.int32, (dvd.shape[0], LANES), 1)
    dv = [jnp.where(lane < 64, dvd[:, (2 * j) * 128:(2 * j + 1) * 128],
                    dvd[:, (2 * j + 1) * 128:(2 * j + 2) * 128]) for j in range(2)]
    if fresh:
        kpt_ref, dkbt_ref, rowt_ref, dkt_ref, dvt_ref = extra[0:5]
        ckv_t, kr_t, dk_t = ckv.T, kr.T, dk.T
        kpt_ref[0:128, :] = ckv_t.astype(BF16)
        kpt_ref[128:256, :] = kr_t.astype(BF16)
        dkbt_ref[...] = dk_t.astype(BF16)
        rowt_ref[0:128, :] = ckv_t
        rowt_ref[128:160, :] = kr_t[0:32, :]
        dkt_ref[...] = dk_t
        for j in range(2):
            dvt_ref[j * 128:(j + 1) * 128, :] = dv[j].T
    else:
        row_ref, dk_ref, dv_ref = extra
        row_ref[:, 0:128] = ckv
        row_ref[:, 128:160] = kr[:, 0:32]
        dk_ref[...] = dk
        for j in range(2):
            dv_ref[:, j * 128:(j + 1) * 128] = dv[j]


def _proj(x, g, w, cos, sin, gq, wuq, wuk, gkv, *, mla_qscale, diff_qscale, seq=None):
    m, d = x.shape
    feature_major = seq is not None
    tm = _tile(seq[2] if feature_major else m, 512)
    row = lambda n: pl.BlockSpec((tm, n), lambda i: (i, 0))
    full = _resident
    outs = [(1536, F32), (512, F32), (128, F32), (256, BF16), (512, BF16), (128, BF16),
            (256, BF16), (512, BF16)]
    out_shape = [jax.ShapeDtypeStruct((m, n), dt) for n, dt in outs]
    out_specs = [row(n) for n, _ in outs]
    ins = [x, g, w, cos, sin, gq, wuq, wuk, gkv]
    in_specs = [row(d), full(g), full(w), row(128), row(128), full(gq), full(wuq),
                full(wuk), full(gkv)]
    aliases = {}
    n_alias = 0
    nt = None
    scratch = []
    if feature_major:
        layer, nb, t, stacked, conv_w = seq
        nt = t // tm
        ins.append(conv_w)
        in_specs.append(full(conv_w))
        out_shape += [jax.ShapeDtypeStruct((nb, 256, t), BF16)] * 2
        out_specs += [pl.BlockSpec((None, 256, tm), lambda i: (i // nt, 0, i % nt))] * 2
        for a in stacked:
            out_shape.append(jax.ShapeDtypeStruct(a.shape, F32))
            out_specs.append(pl.BlockSpec((None, None, a.shape[2], tm),
                                          lambda i: (layer, i // nt, 0, i % nt)))
        n_alias = len(stacked)
        aliases = {len(ins) + k: len(outs) + 2 + k for k in range(n_alias)}
        ins += list(stacked)
        in_specs += [pl.BlockSpec(memory_space=pl.ANY)] * n_alias
        out_shape.append(jax.ShapeDtypeStruct((nb, 8, 1536), F32))
        out_specs.append(pl.BlockSpec((None, 8, 1536), lambda i: (i // nt, 0, 0)))
        scratch = [pltpu.VMEM((8 + tm, 1536), F32)]
    else:
        for n in (160, 256, 256):
            out_shape.append(jax.ShapeDtypeStruct((m, n), F32))
            out_specs.append(row(n))
    kern = functools.partial(_proj_kernel, mla_qscale=mla_qscale, diff_qscale=diff_qscale,
                             n_alias=n_alias, nt=nt)
    return pl.pallas_call(
        kern,
        out_shape=out_shape,
        grid=(m // tm,),
        in_specs=in_specs,
        out_specs=out_specs,
        scratch_shapes=scratch,
        input_output_aliases=aliases,
        compiler_params=_cp(("arbitrary" if feature_major else "parallel",)),
        name="proj_in",
    )(*ins)


def _conv_silu_norm(rows, cw_ref):
    y = rows(0) * cw_ref[0:1, :]
    for i in range(1, CONV_W):
        y = y + rows(i) * cw_ref[i:i + 1, :]
    act = _silu(y)
    parts = []
    for j in range(12):
        x = act[:, j * 128:(j + 1) * 128]
        if j < 8:
            x = x * lax.rsqrt(jnp.sum(x * x, axis=-1, keepdims=True) + EPS)
        if j < 4:
            x = x * (128.0 ** -0.5)
        parts.append(x)
    return parts


def _gdn_kernel(x_ref, z_ref, ba_ref, conv0_ref, s0_ref, cw_ref, alog_ref, dtb_ref, gout_ref,
                *refs, nbb, t_in, t_cmp, t_valid, pre):
    o_ref, sout_ref, xs_scr, s_scr = refs[-4:]
    _gdn_body(x_ref, z_ref, ba_ref, conv0_ref, s0_ref, cw_ref, alog_ref, dtb_ref, gout_ref,
              o_ref, sout_ref, xs_scr, s_scr, nbb=nbb, t_in=t_in, t_cmp=t_cmp, t_valid=t_valid,
              pre=pre)


def _gdn_body(x_ref, z_ref, ba_ref, conv0_ref, s0_ref, cw_ref, alog_ref, dtb_ref, gout_ref,
              o_ref, sout_ref, xs_scr, s_scr, *, nbb, t_in, t_cmp, t_valid, pre):
    C = DN_CHUNK
    nc = t_cmp // C
    tb = pl.program_id(1)

    @pl.when(tb == 0)
    def _():
        s_scr[...] = s0_ref[...]
        if not pre:
            for bb in range(nbb):
                xs_scr[bb, 5:8, :] = conv0_ref[bb]

    if not pre:
        @pl.when(tb != 0)
        def _():
            for bb in range(nbb):
                xs_scr[bb, 5:8, :] = xs_scr[bb, 5 + t_in:8 + t_in, :]

        for bb in range(nbb):
            xs_scr[bb, 8:8 + t_in, :] = x_ref[bb * t_in:(bb + 1) * t_in, :]
            if t_cmp > t_in:
                xs_scr[bb, 8 + t_in:8 + t_cmp, :] = jnp.zeros((t_cmp - t_in, xs_scr.shape[2]),
                                                              F32)

    ri = lax.broadcasted_iota(jnp.int32, (C, C), 0)
    ci = lax.broadcasted_iota(jnp.int32, (C, C), 1)
    incl = ri >= ci
    strict = ri > ci
    tril = jnp.where(incl, 1.0, 0.0).astype(F32)
    alog = alog_ref[...]
    dtb = dtb_ref[...]
    gout = gout_ref[...]

    trip = []
    for bb in range(nbb):
        for c in range(nc):
            r0 = c * C
            if pre:
                parts = [x_ref[bb * t_in + r0:bb * t_in + r0 + C, j * 128:(j + 1) * 128]
                         for j in range(12)]
            else:
                parts = _conv_silu_norm(
                    lambda i: xs_scr[bb, 5 + r0 + i:5 + r0 + i + C, :], cw_ref)
            if t_in >= t_cmp:
                ba = ba_ref[bb * t_in + r0:bb * t_in + r0 + C, :]
            else:
                ba = jnp.concatenate([ba_ref[bb * t_in:(bb + 1) * t_in, :],
                                      jnp.zeros((t_cmp - t_in, LANES), F32)], axis=0)
            bet_all = 1.0 / (1.0 + jnp.exp(-ba))
            xa = ba + dtb
            g_all = -jnp.exp(alog) * (jnp.maximum(xa, 0.0)
                                      + jnp.log(1.0 + jnp.exp(-jnp.abs(xa))))
            if t_valid < t_cmp:
                valid = lax.broadcasted_iota(jnp.int32, (C, 1), 0) + r0 < t_valid
                parts = [jnp.where(valid, x, 0.0) for x in parts]
                bet_all = jnp.where(valid, bet_all, 0.0)
                g_all = jnp.where(valid, g_all, 0.0)
            gam_all = jnp.dot(tril, g_all, preferred_element_type=F32,
                              precision=lax.Precision.HIGHEST)
            gam_t = gam_all.T
            for h in range(4):
                q, k, v = parts[h], parts[4 + h], parts[8 + h]
                bet = bet_all[:, h:h + 1]
                gcol = gam_all[:, 4 + h:5 + h]
                grow = gam_t[4 + h:5 + h, :]
                glast = gam_all[C - 1:C, 4 + h:5 + h]
                diff = gcol - grow
                decay = jnp.where(incl, jnp.exp(jnp.where(incl, diff, 0.0)), 0.0)
                eg = jnp.exp(gcol)
                trip.append(dict(bb=bb, c=c, h=h, q=q, k=k, v=v, bet=bet, decay=decay, eg=eg,
                                 kd=k * jnp.exp(glast - gcol), sdec=jnp.exp(glast)))

    for t in trip:
        t["kk"] = _mm_nt(t["k"], t["k"])
    for t in trip:
        t["qk"] = _mm_nt(t["q"], t["k"]) * t["decay"]
    for t in trip:
        t["tp"] = jnp.where(strict, t["bet"] * t["kk"] * t["decay"], 0.0)
        t["x"] = jnp.concatenate([t["v"] * t["bet"], t["k"] * (t["bet"] * t["eg"])], axis=-1)
    b16r, b16c = lax.shift_right_logical(ri, 4), lax.shift_right_logical(ci, 4)
    b32r, b32c = lax.shift_right_logical(ri, 5), lax.shift_right_logical(ci, 5)
    in16 = b16r == b16c
    in32 = b32r == b32c
    eye = jnp.where(ri == ci, 1.0, 0.0).astype(F32)
    for t in trip:
        t["td"] = jnp.where(in16, t["tp"], 0.0)
    for t in trip:
        t["s2"] = _mm(t["td"], t["td"])
    for t in trip:
        t["s4"] = _mm(t["s2"], t["s2"])
        t["a"] = eye - t["td"] + t["s2"] - _mm(t["td"], t["s2"])
    for t in trip:
        t["s8"] = _mm(t["s4"], t["s4"])
    for t in trip:
        t["bm"] = t["s4"] + t["s8"] + _mm(t["s4"], t["s8"])
    for t in trip:
        t["p"] = t["a"] + _mm(t["a"], t["bm"])
    for t in trip:
        t["lp"] = _mm(jnp.where(in32 & ~in16, t["tp"], 0.0), t["p"])
    for t in trip:
        t["p"] = t["p"] - _mm(t["p"], t["lp"])
    for t in trip:
        t["lp"] = _mm(jnp.where(in32, 0.0, t["tp"]), t["p"])
    for t in trip:
        t["p"] = t["p"] - _mm(t["p"], t["lp"])
    for t in trip:
        t["x"] = _mm(t["p"], t["x"])
    state = {(bb, h): s_scr[bb, h] for bb in range(nbb) for h in range(4)}
    for c in range(nc):
        cur = [t for t in trip if t["c"] == c]
        for t in cur:
            s_h = state[(t["bb"], t["h"])]
            t["ws"] = _mm(t["x"][:, 128:256], s_h)
            t["qs"] = _mm(t["q"], s_h)
        for t in cur:
            t["u"] = t["x"][:, 0:128] - t["ws"]
        for t in cur:
            t["o"] = t["eg"] * t["qs"] + _mm(t["qk"], t["u"])
            key = (t["bb"], t["h"])
            state[key] = t["sdec"] * state[key] + _mm_tn(t["kd"], t["u"])
        r0 = c * C
        if r0 < t_in:
            n = min(C, t_in - r0)
            for t in cur:
                bb, h = t["bb"], t["h"]
                zz = z_ref[bb * t_in + r0:bb * t_in + r0 + n, h * 128:(h + 1) * 128]
                o_ref[bb * t_in + r0:bb * t_in + r0 + n, h * 128:(h + 1) * 128] = (
                    _rms(t["o"][0:n, :], gout) * _silu(zz)).astype(o_ref.dtype)
    for (bb, h), s_h in state.items():
        s_scr[bb, h] = s_h

    @pl.when(tb == pl.num_programs(1) - 1)
    def _():
        sout_ref[...] = s_scr[...]


def _gdn(qkv, z, ba, conv0, s0, layer, cw, alog, dtb, gout, s_stack, *, nb, t, out_layer, pre):
    if pre:
        assert t % DN_CHUNK == 0
        nbb = 1
        t_in = t_cmp = t_valid = _tile(t, 512)
    else:
        assert t <= DN_CHUNK
        nbb = _tile(nb, 4)
        t_in, t_cmp, t_valid = t, DN_CHUNK, t
    nt = t // t_in
    rows = lambda n: pl.BlockSpec((nbb * t_in, n), lambda b, j: (b * nt + j, 0))
    full = lambda a: pl.BlockSpec(a.shape, lambda b, j: (0,) * a.ndim)
    kern = functools.partial(_gdn_kernel, nbb=nbb, t_in=t_in, t_cmp=t_cmp, t_valid=t_valid,
                             pre=pre)
    ins = [qkv, z, ba, conv0, s0, cw, alog, dtb, gout, s_stack]
    in_specs = [rows(1536), rows(512), rows(128),
                pl.BlockSpec((None, nbb, 3, 1536), lambda b, j: (layer, b, 0, 0)),
                pl.BlockSpec((None, nbb, 4, 128, 128), lambda b, j: (layer, b, 0, 0, 0)),
                full(cw), full(alog), full(dtb), full(gout), pl.BlockSpec(memory_space=pl.ANY)]
    return pl.pallas_call(
        kern,
        out_shape=[jax.ShapeDtypeStruct((nb * t, 512), BF16 if t_in % 16 == 0 else F32),
                   jax.ShapeDtypeStruct(s_stack.shape, F32)],
        grid=(nb // nbb, nt),
        in_specs=in_specs,
        out_specs=[rows(512), pl.BlockSpec((None, nbb, 4, 128, 128),
                                           lambda b, j: (out_layer, b, 0, 0, 0))],
        scratch_shapes=[pltpu.VMEM((nbb, 8 + t_cmp, 1536), F32),
                        pltpu.VMEM((nbb, 4, 128, 128), F32)],
        input_output_aliases={len(ins) - 1: 1},
        compiler_params=_cp(("parallel", "arbitrary")),
        name="gdn",
    )(*ins)


def _causal_sweep(i, tq, tk, nheads, step):
    n_full = (i * tq) // tk
    n_tot = ((i + 1) * tq + tk - 1) // tk

    def full_body(j, carry):
        step(j, None)
        return carry

    lax.fori_loop(0, n_full, full_body, 0)

    def diag_body(j, carry):
        qpos = i * tq + lax.broadcasted_iota(jnp.int32, (tq, tk), 0)
        kpos = j * tk + lax.broadcasted_iota(jnp.int32, (tq, tk), 1)
        mask = kpos <= qpos
        step(j, jnp.concatenate([mask] * nheads, axis=0))
        return carry

    lax.fori_loop(n_full, n_tot, diag_body, 0)


def _online_softmax(s, m_scr, tk):
    m_prev = m_scr[...]
    m_next = jnp.maximum(m_prev, jnp.max(s, axis=-1, keepdims=True))
    p = jnp.exp2(s - _rep(m_next, tk // LANES)).astype(BF16)
    alpha = jnp.exp2(m_prev - m_next)
    m_scr[...] = m_next
    return p, alpha


def _mla_attn_kernel(qlat_ref, qrope_ref, kp_ref, kpt_ref, wuv_ref, o_ref, q_scr, m_scr, acc_scr,
                     *, tq, tk):
    i = pl.program_id(1)
    qr = qrope_ref[...]
    lane = lax.broadcasted_iota(jnp.int32, qr.shape, 1)
    for h in range(4):
        q_scr[h * tq:(h + 1) * tq, 0:128] = qlat_ref[:, h * 128:(h + 1) * 128]
        q_scr[h * tq:(h + 1) * tq, 128:256] = jnp.where(_lane_group(lane, 32, h), qr,
                                                        jnp.zeros_like(qr))
    m_scr[...] = jnp.full(m_scr.shape, -jnp.inf, F32)
    acc_scr[...] = jnp.zeros(acc_scr.shape, F32)
    ones = jnp.ones((tk, LANES), BF16)

    def step(j, mask):
        start = pl.multiple_of(j * tk, tk)
        s = jnp.dot(q_scr[...], kpt_ref[:, pl.ds(start, tk)], preferred_element_type=F32)
        if mask is not None:
            s = jnp.where(mask, s, NEG)
        p, alpha = _online_softmax(s, m_scr, tk)
        vext = jnp.concatenate([kp_ref[pl.ds(start, tk), 0:128], ones], axis=1)
        acc_scr[...] = acc_scr[...] * _rep(alpha, 2) + jnp.dot(
            p, vext, preferred_element_type=F32)

    _causal_sweep(i, tq, tk, 4, step)
    o = (acc_scr[:, 0:128] / acc_scr[:, 128:256]).astype(BF16)
    bo = jnp.dot(o[0:tq], wuv_ref[0:128, :], preferred_element_type=F32)
    for h in range(1, 4):
        bo = bo + jnp.dot(o[h * tq:(h + 1) * tq], wuv_ref[h * 128:(h + 1) * 128, :],
                          preferred_element_type=F32)
    o_ref[...] = bo.astype(BF16)


def _mla_attn(qlat, qrope, kp, kpt, wuv, *, nb, t):
    tq = _tile(t, 512)
    tk = _tile(t, 512)
    nq = t // tq
    kern = functools.partial(_mla_attn_kernel, tq=tq, tk=tk)
    return pl.pallas_call(
        kern,
        out_shape=jax.ShapeDtypeStruct((nb * t, 256), BF16),
        grid=(nb, nq),
        in_specs=[pl.BlockSpec((tq, 512), lambda b, i: (b * nq + i, 0)),
                  pl.BlockSpec((tq, 128), lambda b, i: (b * nq + i, 0)),
                  pl.BlockSpec((t, 256), lambda b, i: (b, 0)),
                  pl.BlockSpec((None, 256, t), lambda b, i: (b, 0, 0)),
                  pl.BlockSpec(wuv.shape, lambda b, i: (0, 0))],
        out_specs=pl.BlockSpec((tq, 256), lambda b, i: (b * nq + i, 0)),
        scratch_shapes=[pltpu.VMEM((4 * tq, 256), BF16), pltpu.VMEM((4 * tq, LANES), F32),
                        pltpu.VMEM((4 * tq, 256), F32)],
        compiler_params=_cp(("parallel", "parallel")),
        name="mla_attn",
    )(qlat, qrope, kp, kpt, wuv)


def _diff_lambda(lq1, lk1, lq2, lk2, lam_init):
    return (jnp.exp(jnp.sum(lq1 * lk1, axis=-1, keepdims=True))
            - jnp.exp(jnp.sum(lq2 * lk2, axis=-1, keepdims=True)) + lam_init)


def _diff_attn_kernel(q_ref, kt_ref, v_ref, lq1_ref, lk1_ref, lq2_ref, lk2_ref, gsub_ref, o_ref,
                      q_scr, m_scr, acc_scr, *, tq, tk, lam_init):
    i = pl.program_id(1)
    q = q_ref[...]
    lane = lax.broadcasted_iota(jnp.int32, q.shape, 1)
    for h in range(8):
        q_scr[h * tq:(h + 1) * tq, :] = jnp.where(_lane_group(lane, 32, h), q, jnp.zeros_like(q))
    m_scr[...] = jnp.full(m_scr.shape, -jnp.inf, F32)
    acc_scr[...] = jnp.zeros(acc_scr.shape, F32)
    ones = jnp.ones((tk, LANES), BF16)

    def step(j, mask):
        start = pl.multiple_of(j * tk, tk)
        s = jnp.dot(q_scr[...], kt_ref[:, pl.ds(start, tk)], preferred_element_type=F32)
        if mask is not None:
            s = jnp.where(mask, s, NEG)
        p, alpha = _online_softmax(s, m_scr, tk)
        alpha = _rep(alpha, 2)
        for c in range(4):
            vext = jnp.concatenate([v_ref[pl.ds(start, tk), c * 128:(c + 1) * 128], ones], axis=1)
            r = slice(2 * c * tq, (2 * c + 2) * tq)
            acc_scr[r, :] = acc_scr[r, :] * alpha[r, :] + jnp.dot(
                p[r, :], vext, preferred_element_type=F32)

    _causal_sweep(i, tq, tk, 8, step)
    lam = _diff_lambda(lq1_ref[...], lk1_ref[...], lq2_ref[...], lk2_ref[...], lam_init)
    on = acc_scr[:, 0:128] / acc_scr[:, 128:256]
    lane128 = lax.broadcasted_iota(jnp.int32, (tq, LANES), 1)
    outs = []
    for c in range(4):
        d = on[2 * c * tq:(2 * c + 1) * tq] - lam * on[(2 * c + 1) * tq:(2 * c + 2) * tq]
        outs.append(_rms(d, gsub_ref[...]) * (1.0 - lam_init))
    for j in range(2):
        o_ref[:, j * 128:(j + 1) * 128] = jnp.where(
            lane128 < 64, outs[2 * j], outs[2 * j + 1]).astype(BF16)


def _diff_attn(q, kt, vdup, lq1, lk1, lq2, lk2, gsub, *, nb, t, lam_init):
    tq = _tile(t, 256)
    tk = _tile(t, 512)
    nq = t // tq
    kern = functools.partial(_diff_attn_kernel, tq=tq, tk=tk, lam_init=lam_init)
    vec = lambda a: pl.BlockSpec(a.shape, lambda b, i: (0, 0))
    return pl.pallas_call(
        kern,
        out_shape=jax.ShapeDtypeStruct((nb * t, 256), BF16),
        grid=(nb, nq),
        in_specs=[pl.BlockSpec((tq, 256), lambda b, i: (b * nq + i, 0)),
                  pl.BlockSpec((None, 256, t), lambda b, i: (b, 0, 0)),
                  pl.BlockSpec((t, 512), lambda b, i: (b, 0)),
                  vec(lq1), vec(lk1), vec(lq2), vec(lk2), vec(gsub)],
        out_specs=pl.BlockSpec((tq, 256), lambda b, i: (b * nq + i, 0)),
        scratch_shapes=[pltpu.VMEM((8 * tq, 256), BF16), pltpu.VMEM((8 * tq, LANES), F32),
                        pltpu.VMEM((8 * tq, 256), F32)],
        compiler_params=_cp(("parallel", "parallel")),
        name="diff_attn",
    )(q, kt, vdup, lq1, lk1, lq2, lk2, gsub)


def _paged_kernel(pt_ref, mla_hbm, dk_hbm, dv_hbm, qlat_ref, qr_ref, dq_ref, row_ref, dkn_ref,
                  dvn_ref, wuv_ref, lq1_ref, lk1_ref, lq2_ref, lk2_ref, gsub_ref, bo_ref, co_ref,
                  mla_buf, dk_buf, dv_buf, sem, dqb_scr, ma_scr, la_scr, aa_scr, md_scr, ld_scr,
                  ad_scr, *, g, nslot, ts, layer, lam_init):
    b = pl.program_id(0)
    nb = pl.num_programs(0)

    def copies(seq, slot):
        out = []
        for gi in range(g):
            pg = pt_ref[seq, slot * g + gi]
            for hbm, buf in ((mla_hbm, mla_buf), (dk_hbm, dk_buf), (dv_hbm, dv_buf)):
                out.append(pltpu.make_async_copy(hbm.at[layer, pg], buf.at[slot, gi],
                                                 sem.at[slot]))
        return out

    last = b == nb - 1
    nxt_b = jnp.where(last, b, b + 1)

    @pl.when(b == 0)
    def _():
        for k in range(nslot):
            for c in copies(b, k):
                c.start()

    q = dq_ref[...]
    lane = lax.broadcasted_iota(jnp.int32, q.shape, 1)
    for h in range(8):
        dqb_scr[h * ts:(h + 1) * ts, :] = jnp.where(_lane_group(lane, 32, h), q, 0.0)
    ma_scr[...] = jnp.full(ma_scr.shape, -jnp.inf, F32)
    la_scr[...] = jnp.zeros(la_scr.shape, F32)
    aa_scr[...] = jnp.zeros(aa_scr.shape, F32)
    md_scr[...] = jnp.full(md_scr.shape, -jnp.inf, F32)
    ld_scr[...] = jnp.zeros(ld_scr.shape, F32)
    ad_scr[...] = jnp.zeros(ad_scr.shape, F32)

    qlat = qlat_ref[...].astype(BF16)
    qr = qr_ref[...].astype(BF16)
    dqb = dqb_scr[...].astype(BF16)

    def update(s_list, pv, m_scr, l_scr, acc_scr):
        s = jnp.concatenate(s_list, axis=-1) if len(s_list) > 1 else s_list[0]
        m_old = m_scr[...]
        m_new = jnp.maximum(m_old, jnp.max(s, axis=-1, keepdims=True))
        p = jnp.exp2(s - m_new)
        corr = jnp.exp2(m_old - m_new)
        l_scr[...] = l_scr[...] * corr + jnp.sum(p, axis=-1, keepdims=True)
        pb = p.astype(BF16)
        acc = acc_scr[...] * corr
        off = 0
        for idx, sb in enumerate(s_list):
            n = sb.shape[1]
            acc = acc + pv(pb[:, off:off + n], idx)
            off += n
        acc_scr[...] = acc
        m_scr[...] = m_new

    def consume(slot):
        kts = [mla_buf[slot, gi].astype(BF16) for gi in range(g)]
        sa = [jnp.dot(qlat, kt[0:128, :], preferred_element_type=F32)
              + jnp.dot(qr, kt[128:160, :], preferred_element_type=F32) for kt in kts]
        update(sa, lambda pb, idx: _mm_nt(pb, kts[idx][0:128, :]), ma_scr, la_scr, aa_scr)
        sd = [jnp.dot(dqb, dk_buf[slot, gi].astype(BF16), preferred_element_type=F32)
              for gi in range(g)]
        update(sd, lambda pb, idx: _mm_nt(pb, dv_buf[slot, idx]), md_scr, ld_scr, ad_scr)

    for k in range(nslot):
        for c in copies(b, k):
            c.wait()
        consume(k)
        for c in copies(nxt_b, k):
            c.start()

    @pl.when(last)
    def _():
        for k in range(nslot):
            for c in copies(nxt_b, k):
                c.wait()

    def finalize():
        tsp = -(-ts // 16) * 16
        qpos = lax.broadcasted_iota(jnp.int32, (ts, tsp), 0)
        kpos = lax.broadcasted_iota(jnp.int32, (ts, tsp), 1)
        mask = kpos <= qpos

        def padded(ref):
            rows = ref[...]
            return jnp.concatenate([rows, jnp.zeros((tsp - ts, rows.shape[1]), F32)],
                                   axis=0).astype(BF16)

        rows = padded(row_ref)
        s = _mm_nt(qlat, rows[:, 0:128]) + _mm_nt(qr, rows[:, 128:160])
        update([jnp.where(jnp.concatenate([mask] * 4, axis=0), s, NEG)],
               lambda pb, idx: jnp.dot(pb, rows[:, 0:128], preferred_element_type=F32),
               ma_scr, la_scr, aa_scr)
        s = _mm_nt(dqb, padded(dkn_ref))
        vn = padded(dvn_ref)
        update([jnp.where(jnp.concatenate([mask] * 8, axis=0), s, NEG)],
               lambda pb, idx: jnp.dot(pb, vn, preferred_element_type=F32),
               md_scr, ld_scr, ad_scr)
        o = (aa_scr[...] / la_scr[...]).astype(BF16)
        bo = jnp.zeros((ts, 256), F32)
        for h in range(4):
            full = jnp.dot(o, wuv_ref[h * 128:(h + 1) * 128, :], preferred_element_type=F32)
            bo = bo + full[h * ts:(h + 1) * ts]
        bo_ref[...] = bo
        on = ad_scr[...] / ld_scr[...]
        lam = _diff_lambda(lq1_ref[...], lk1_ref[...], lq2_ref[...], lk2_ref[...], lam_init)
        lane = lax.broadcasted_iota(jnp.int32, (ts, 256), 1)
        d = jnp.zeros((ts, 256), F32)
        for c in range(4):
            dc = on[2 * c * ts:(2 * c + 1) * ts] - lam * on[(2 * c + 1) * ts:(2 * c + 2) * ts]
            d = jnp.where(_lane_group(lane, 64, c), dc, d)
        ms = jnp.zeros((ts, 256), F32)
        for c in range(4):
            sel = _lane_group(lane, 64, c)
            ssq = jnp.sum(jnp.where(sel, d * d, 0.0), axis=-1, keepdims=True) * (1.0 / 64)
            ms = jnp.where(sel, ssq, ms)
        co_ref[...] = d * lax.rsqrt(ms + EPS) * gsub_ref[...] * (1.0 - lam_init)

    finalize()


def _paged(page_table, cache_mla_t, cache_dk_t, cache_dv_t, layer, qlat, qr, dq, row, dkn, dvn,
           wuv, lq1, lk1, lq2, lk2, gsub4, *, nb, ts, lam_init):
    n_pages = page_table.shape[1]
    g = _tile(n_pages, 16)
    nslot = n_pages // g
    kern = functools.partial(_paged_kernel, g=g, nslot=nslot, ts=ts, layer=layer,
                             lam_init=lam_init)
    per_b = lambda r, n: pl.BlockSpec((None, r, n), lambda b, pt: (b, 0, 0))
    vec = lambda a: pl.BlockSpec(a.shape, lambda b, pt: (0, 0))
    hbm = pl.BlockSpec(memory_space=pl.ANY)
    in_specs = [hbm, hbm, hbm,
                per_b(4 * ts, 128), per_b(4 * ts, 32), per_b(ts, 256), per_b(ts, 160),
                per_b(ts, 256), per_b(ts, 256), vec(wuv), vec(lq1), vec(lk1), vec(lq2),
                vec(lk2), vec(gsub4)]
    grid_spec = pltpu.PrefetchScalarGridSpec(
        num_scalar_prefetch=1,
        grid=(nb,),
        in_specs=in_specs,
        out_specs=[per_b(ts, 256), per_b(ts, 256)],
        scratch_shapes=[pltpu.VMEM((nslot, g, 160, PAGE), F32),
                        pltpu.VMEM((nslot, g, 256, PAGE), F32),
                        pltpu.VMEM((nslot, g, 256, PAGE), F32),
                        pltpu.SemaphoreType.DMA((nslot,)),
                        pltpu.VMEM((8 * ts, 256), F32),
                        pltpu.VMEM((4 * ts, 1), F32), pltpu.VMEM((4 * ts, 1), F32),
                        pltpu.VMEM((4 * ts, 128), F32),
                        pltpu.VMEM((8 * ts, 1), F32), pltpu.VMEM((8 * ts, 1), F32),
                        pltpu.VMEM((8 * ts, 256), F32)])
    return pl.pallas_call(
        kern,
        out_shape=[jax.ShapeDtypeStruct((nb, ts, 256), F32),
                   jax.ShapeDtypeStruct((nb, ts, 256), F32)],
        grid_spec=grid_spec,
        compiler_params=_cp(("arbitrary",)),
        name="paged_attn",
    )(page_table, cache_mla_t, cache_dk_t, cache_dv_t,
      qlat, qr, dq, row, dkn, dvn, wuv, lq1, lk1, lq2, lk2, gsub4)


def _pack_w_in(w):
    d = w.shape[0]
    a_qkv, a_z, a_b, a_a, b_cq, b_ckv, b_kr, c_q, c_k, c_v = jnp.split(
        w, np.cumsum([1536, 512, 4, 4, 192, 128, 32, 256, 256]).tolist(), axis=1)
    z = lambda n: jnp.zeros((d, n), w.dtype)
    kr_sw = jnp.concatenate([-b_kr[:, 16:32], b_kr[:, 0:16]], axis=1)
    v4 = c_v.reshape(d, 4, 64)
    vdup = jnp.concatenate([v4, v4], axis=2).reshape(d, 512)
    return jnp.concatenate(
        [a_qkv, a_z, a_b, a_a, z(120), b_ckv, b_cq, z(64), jnp.tile(b_kr, (1, 4)),
         jnp.tile(kr_sw, (1, 4)), c_q, c_k, vdup], axis=1).astype(BF16)


def _pack_mla(w_uq, w_uk, w_uv):
    nope = w_uq[:, :, 0:64].reshape(192, 256)
    rope = w_uq[:, :, 64:96]
    rope_sw = jnp.concatenate([-rope[:, :, 16:32], rope[:, :, 0:16]], axis=2)
    wuq = jnp.concatenate([nope, rope.reshape(192, 128), rope_sw.reshape(192, 128)], axis=1)
    wuq = jnp.concatenate([wuq, jnp.zeros((64, 512), wuq.dtype)], axis=0)
    eye = jnp.eye(4, dtype=w_uk.dtype)
    wuk_bd = jnp.einsum('rhn,hg->hngr', w_uk, eye).reshape(256, 512)
    wuv_bd = jnp.einsum('rhe,hg->hrge', w_uv, eye).reshape(512, 256)
    return wuq.astype(BF16), wuk_bd.astype(BF16), wuv_bd.astype(BF16)


def _rope_tables(pos):
    freqs = jnp.power(ROPE_THETA, -jnp.arange(16, dtype=F32) / 16)
    ang = pos[:, None] * freqs[None, :]
    return jnp.tile(jnp.cos(ang), (1, 8)), jnp.tile(jnp.sin(ang), (1, 8))


def _lane_pad(v, n=LANES, offset=0):
    v = v.reshape(1, -1).astype(F32)
    return jnp.pad(v, ((0, 0), (offset, n - offset - v.shape[1])))


def kernel(x_prompt, x_sample, cache_mla, cache_diff_k, cache_diff_v, state_dn_S, state_dn_conv, page_table, g_f1_pre, g_f1_post, w_f1_gate, w_f1_up, w_f1_down, g_mix_pre, g_mix_post, w_in, w_out, dn_conv_w, dn_A_log, dn_dt_bias, dn_g_out, mla_g_q, mla_w_uq, mla_g_kv, mla_w_uk, mla_w_uv, diff_lam_q1, diff_lam_k1, diff_lam_q2, diff_lam_k2, diff_g_subln, g_f2_pre, g_f2_post, w_f2_gate, w_f2_up, w_f2_down):
    bp, tp, d = x_prompt.shape
    bs, ts, _ = x_sample.shape
    depth = w_in.shape[0]
    n_pages = page_table.shape[1]
    past_len = n_pages * PAGE
    mla_qscale = 96.0 ** -0.5 * LOG2E
    diff_qscale = 32.0 ** -0.5 * LOG2E

    cos_p, sin_p = _rope_tables(jnp.arange(tp, dtype=F32))
    cos_p, sin_p = jnp.tile(cos_p, (bp, 1)), jnp.tile(sin_p, (bp, 1))
    cos_s, sin_s = _rope_tables(past_len + jnp.arange(ts, dtype=F32))
    cos_s, sin_s = jnp.tile(cos_s, (bs, 1)), jnp.tile(sin_s, (bs, 1))

    n_pool = cache_mla.shape[1]
    cache_mla_t = jnp.swapaxes(cache_mla, 2, 3)
    cache_dk_t = jnp.transpose(cache_diff_k, (0, 1, 3, 4, 2)).reshape(depth, n_pool, 256, PAGE)
    cache_dv_t = jnp.transpose(cache_diff_v, (0, 1, 3, 4, 2)).reshape(depth, n_pool, 256, PAGE)
    zero_conv = jnp.zeros((1, bp, CONV_W - 1, 1536), F32)
    zero_s = jnp.zeros((1, bp, 4, 128, 128), F32)

    xp = x_prompt.reshape(bp * tp, d)
    xs = x_sample.reshape(bs * ts, d)
    p_conv = []
    p_fm = [jnp.zeros((depth, bp, n, tp), F32) for n in (160, 256, 256)]
    p_s = jnp.zeros((depth, bp, 4, 128, 128), F32)
    s_s = jnp.zeros((depth, bs, 4, 128, 128), F32)
    s_out = [[] for _ in range(4)]
    row1 = lambda v: v.reshape(1, -1).astype(F32)
    for l in range(depth):
        lam_init = 0.8 - 0.6 * math.exp(-0.3 * l)
        wg1, wu1, wd1 = (w_f1_gate[l].astype(BF16), w_f1_up[l].astype(BF16),
                         w_f1_down[l].astype(BF16))
        wg2, wu2, wd2 = (w_f2_gate[l].astype(BF16), w_f2_up[l].astype(BF16),
                         w_f2_down[l].astype(BF16))
        w_inp = _pack_w_in(w_in[l])
        wuq, wuk_bd, wuv_bd = _pack_mla(mla_w_uq[l], mla_w_uk[l], mla_w_uv[l])
        w_o = w_out[l].astype(BF16)
        gq = _lane_pad(mla_g_q[l], 256)
        gkv = row1(mla_g_kv[l])
        alog = _lane_pad(dn_A_log[l], offset=4)
        dtb = _lane_pad(dn_dt_bias[l], offset=4)
        gout = row1(dn_g_out[l])
        cw = dn_conv_w[l].astype(F32)
        lq1, lk1, lq2, lk2 = (_lane_pad(diff_lam_q1[l]), _lane_pad(diff_lam_k1[l]),
                              _lane_pad(diff_lam_q2[l]), _lane_pad(diff_lam_k2[l]))
        gsub2 = jnp.tile(row1(diff_g_subln[l]), (1, 2))
        gsub4 = jnp.tile(row1(diff_g_subln[l]), (1, 4))

        def front(x, cos, sin, seq=None):
            x = _ffn(x, row1(g_f1_pre[l]), wg1, wu1, wd1, row1(g_f1_post[l]))
            return x, _proj(x, row1(g_mix_pre[l]), w_inp, cos, sin, gq, wuq, wuk_bd, gkv,
                            mla_qscale=mla_qscale, diff_qscale=diff_qscale, seq=seq)

        def back(x, ao, bo, co):
            return _ffn(x, row1(g_f2_pre[l]), wg2, wu2, wd2, row1(g_f2_post[l]),
                        mix=(ao, bo, co, w_o, row1(g_mix_post[l])))

        xp, (qkv, z, ba, kp, qlat, qrope, dq, dvb, kpt, dkbt, *p_fm, tail) = front(
            xp, cos_p, sin_p, seq=(l, bp, tp, p_fm, cw))
        ao, p_s = _gdn(qkv, z, ba, zero_conv, zero_s, 0, cw, alog, dtb, gout, p_s, nb=bp, t=tp,
                       out_layer=l, pre=True)
        bo = _mla_attn(qlat, qrope, kp, kpt, wuv_bd, nb=bp, t=tp)
        co = _diff_attn(dq, dkbt, dvb, lq1, lk1, lq2, lk2, gsub2, nb=bp, t=tp, lam_init=lam_init)
        xp = back(xp, ao, bo, co)
        p_conv.append(tail[:, 8 - (CONV_W - 1):])

        xs, (qkv, z, ba, kp, qlat, qrope, dq, dvb, mrow, dk, dv) = front(xs, cos_s, sin_s)
        ao, s_s = _gdn(qkv, z, ba, state_dn_conv, state_dn_S, l, cw, alog, dtb, gout, s_s,
                       nb=bs, t=ts, out_layer=l, pre=False)
        qlat_h = qlat.reshape(bs, ts, 4, 128).transpose(0, 2, 1, 3).reshape(bs, 4 * ts, 128)
        qr_h = qrope.reshape(bs, ts, 4, 32).transpose(0, 2, 1, 3).reshape(bs, 4 * ts, 32)
        bo, co = _paged(page_table, cache_mla_t, cache_dk_t, cache_dv_t, l, qlat_h, qr_h,
                        dq.reshape(bs, ts, 256).astype(F32), mrow.reshape(bs, ts, 160),
                        dk.reshape(bs, ts, 256), dv.reshape(bs, ts, 256), wuv_bd,
                        lq1, lk1, lq2, lk2, gsub4, nb=bs, ts=ts, lam_init=lam_init)
        xs = back(xs, ao, bo.reshape(bs * ts, 256), co.reshape(bs * ts, 256))
        s_out[0].append(qkv.reshape(bs, ts, 1536)[:, ts - (CONV_W - 1):])
        s_out[1].append(mrow.reshape(bs, ts, 160))
        s_out[2].append(dk.reshape(bs, ts, 8, 32))
        s_out[3].append(dv.reshape(bs, ts, 4, 64))

    rowt, dkt, dvt = p_fm
    p_mla = jnp.swapaxes(rowt, 2, 3)
    p_dk = jnp.transpose(dkt.reshape(depth, bp, 8, 32, tp), (0, 1, 4, 2, 3))
    p_dv = jnp.transpose(dvt.reshape(depth, bp, 4, 64, tp), (0, 1, 4, 2, 3))
    s_conv, s_mla, s_dk, s_dv = [jnp.stack(a, axis=0) for a in s_out]
    return (xp.reshape(bp, tp, d), xs.reshape(bs, ts, d), p_mla, p_dk, p_dv, p_s,
            jnp.stack(p_conv, axis=0), s_mla, s_dk, s_dv, s_s, s_conv)
```

```python
import functools
import math

import jax
import jax.numpy as jnp
import numpy as np
from jax import lax
from jax.experimental import pallas as pl
from jax.experimental.pallas import tpu as pltpu

F32 = jnp.float32
BF16 = jnp.bfloat16
EPS = 1e-6
ROPE_THETA = 10000.0
CONV_W = 4
DN_CHUNK = 64
PAGE = 128
LANES = 128
LOG2E = math.log2(math.e)
MXU_N = 256
VMEM_LIMIT = 52 * 1024 * 1024
NEG = -1e30

_P_QKV, _P_Z, _P_BA, _P_CQ, _P_KR, _P_DQ, _P_DK, _P_DV, _P_END = (
    0, 1536, 2048, 2304, 2560, 2816, 3072, 3328, 3840)


def _cp(sem):
    return pltpu.CompilerParams(dimension_semantics=sem, vmem_limit_bytes=VMEM_LIMIT)


def _tile(m, pref):
    t = pref
    while m % t:
        t //= 2
    return t


def _mm(a, b):
    return jnp.dot(a.astype(BF16), b.astype(BF16), preferred_element_type=F32)


def _mm_nt(a, b):
    return lax.dot_general(a.astype(BF16), b.astype(BF16), (((1,), (1,)), ((), ())),
                           preferred_element_type=F32)


def _mm_tn(a, b):
    return lax.dot_general(a.astype(BF16), b.astype(BF16), (((0,), (0,)), ((), ())),
                           preferred_element_type=F32)


def _rms(x, g, n=None):
    n = x.shape[-1] if n is None else n
    ms = jnp.sum(x * x, axis=-1, keepdims=True) * (1.0 / n)
    return x * lax.rsqrt(ms + EPS) * g


def _silu(x):
    return x / (1.0 + jnp.exp(-x))


def _lane_group(lane, width, idx):
    return (lane >= idx * width) & (lane < (idx + 1) * width)


def _rep(x, k):
    return x if k == 1 else jnp.concatenate([x] * k, axis=1)


def _ffn_body(x, gpre_ref, wg_ref, wu_ref, wd_ref, gpost_ref, act_scr):
    h = _rms(x, gpre_ref[...]).astype(BF16)
    ff = wg_ref.shape[1]
    cw = MXU_N
    for c in range(ff // cw):
        a = jnp.dot(h, wg_ref[:, c * cw:(c + 1) * cw], preferred_element_type=F32)
        b = jnp.dot(h, wu_ref[:, c * cw:(c + 1) * cw], preferred_element_type=F32)
        act_scr[:, c * cw:(c + 1) * cw] = (_silu(a) * b).astype(BF16)
    y = jnp.dot(act_scr[...], wd_ref[...], preferred_element_type=F32)
    return x + 0.5 * _rms(y, gpost_ref[...])


def _ffn_kernel(x_ref, gpre_ref, wg_ref, wu_ref, wd_ref, gpost_ref, o_ref, act_scr):
    o_ref[...] = _ffn_body(x_ref[...], gpre_ref, wg_ref, wu_ref, wd_ref, gpost_ref, act_scr)


def _outproj_ffn_kernel(x_ref, ao_ref, bo_ref, co_ref, wo_ref, gmix_ref, gpre_ref, wg_ref, wu_ref,
                        wd_ref, gpost_ref, o_ref, act_scr):
    y = (_mm(ao_ref[...], wo_ref[0:512, :]) + _mm(bo_ref[...], wo_ref[512:768, :])
         + _mm(co_ref[...], wo_ref[768:1024, :]))
    x = x_ref[...] + _rms(y, gmix_ref[...])
    o_ref[...] = _ffn_body(x, gpre_ref, wg_ref, wu_ref, wd_ref, gpost_ref, act_scr)


def _resident(a):
    return pl.BlockSpec(a.shape, lambda i: (0,) * a.ndim, pipeline_mode=pl.Buffered(1))


def _ffn(x, gpre, wg, wu, wd, gpost, mix=None):
    m, d = x.shape
    ff = wg.shape[1]
    assert ff % MXU_N == 0
    tm = _tile(m, 512)
    row = lambda n: pl.BlockSpec((tm, n), lambda i: (i, 0))
    weights = [gpre, wg, wu, wd, gpost]
    if mix is None:
        kern, ins, in_specs = _ffn_kernel, [x], [row(d)]
    else:
        ao, bo, co, wo, gmix = mix
        kern, ins = _outproj_ffn_kernel, [x, ao, bo, co, wo, gmix]
        in_specs = [row(d), row(512), row(256), row(256), _resident(wo), _resident(gmix)]
    return pl.pallas_call(
        kern,
        out_shape=jax.ShapeDtypeStruct((m, d), F32),
        grid=(m // tm,),
        in_specs=in_specs + [_resident(w) for w in weights],
        out_specs=row(d),
        scratch_shapes=[pltpu.VMEM((tm, ff), BF16)],
        compiler_params=_cp(("parallel",)),
        name="ffn" if mix is None else "outproj_ffn",
    )(*ins, *weights)


def _proj_kernel(x_ref, g_ref, w_ref, cos_ref, sin_ref, gq_ref, wuq_ref, wuk_ref, gkv_ref,
                 *refs, mla_qscale, diff_qscale, n_alias, nt):
    fresh = nt is not None
    if fresh:
        cw_ref = refs[0]
        refs = refs[1 + n_alias:]
    qkv_ref, z_ref, ba_ref, kp_ref, qlat_ref, qrope_ref, dq_ref, dvb_ref = refs[0:8]
    extra = refs[8:]
    h = _rms(x_ref[...], g_ref[...]).astype(BF16)

    def proj(lo, hi):
        return jnp.dot(h, w_ref[:, lo:hi], preferred_element_type=F32)

    if fresh:
        tail_ref, xs_scr = extra[5:]
        tm = x_ref.shape[0]

        @pl.when(lax.rem(pl.program_id(0), nt) == 0)
        def _():
            xs_scr[0:8, :] = jnp.zeros((8, xs_scr.shape[1]), F32)

        xs_scr[8:8 + tm, :] = proj(_P_QKV, _P_Z)
        rc = min(tm, 128)
        for r0 in range(0, tm, rc):
            parts = _conv_silu_norm(lambda i: xs_scr[5 + r0 + i:5 + r0 + i + rc, :], cw_ref)
            for j, part in enumerate(parts):
                qkv_ref[r0:r0 + rc, j * 128:(j + 1) * 128] = part
        tail = xs_scr[tm:tm + 8, :]
        tail_ref[...] = tail
        xs_scr[0:8, :] = tail
    else:
        qkv_ref[...] = proj(_P_QKV, _P_Z)
    z_ref[...] = proj(_P_Z, _P_BA)
    ba_ckv = proj(_P_BA, _P_CQ)
    ba_ref[...] = ba_ckv[:, 0:128]
    cos = cos_ref[...]
    sin = sin_ref[...]
    cq = _rms(proj(_P_CQ, _P_KR), gq_ref[...], n=192)
    qb = _mm(cq, wuq_ref[...])
    qlat_ref[...] = (_mm(qb[:, 0:256], wuk_ref[...]) * mla_qscale).astype(BF16)
    qrope_ref[...] = ((qb[:, 256:384] * cos + qb[:, 384:512] * sin) * mla_qscale).astype(BF16)
    ckv = _rms(ba_ckv[:, 128:256], gkv_ref[...])
    kr2 = proj(_P_KR, _P_DQ)
    kr = kr2[:, 0:128] * cos + kr2[:, 128:256] * sin
    kp_ref[:, 0:128] = ckv.astype(BF16)
    kp_ref[:, 128:256] = kr.astype(BF16)
    dq_ref[...] = (proj(_P_DQ, _P_DK) * diff_qscale).astype(BF16)
    dk = proj(_P_DK, _P_DV)
    dvd = proj(_P_DV, _P_END)
    dvb_ref[...] = dvd.astype(BF16)
    lane = lax.broadcasted_iota(jnp.int32, (dvd.shape[0], LANES), 1)
    dv = [jnp.where(lane < 64, dvd[:, (2 * j) * 128:(2 * j + 1) * 128],
                    dvd[:, (2 * j + 1) * 128:(2 * j + 2) * 128]) for j in range(2)]
    if fresh:
        kpt_ref, dkbt_ref, rowt_ref, dkt_ref, dvt_ref = extra[0:5]
        ckv_t, kr_t, dk_t = ckv.T, kr.T, dk.T
        kpt_ref[0:128, :] = ckv_t.astype(BF16)
        kpt_ref[128:256, :] = kr_t.astype(BF16)
        dkbt_ref[...] = dk_t.astype(BF16)
        rowt_ref[0:128, :] = ckv_t
        rowt_ref[128:160, :] = kr_t[0:32, :]
        dkt_ref[...] = dk_t
        for j in range(2):
            dvt_ref[j * 128:(j + 1) * 128, :] = dv[j].T
    else:
        row_ref, dk_ref, dv_ref = extra
        row_ref[:, 0:128] = ckv
        row_ref[:, 128:160] = kr[:, 0:32]
        dk_ref[...] = dk
        for j in range(2):
            dv_ref[:, j * 128:(j + 1) * 128] = dv[j]


def _proj(x, g, w, cos, sin, gq, wuq, wuk, gkv, *, mla_qscale, diff_qscale, seq=None):
    m, d = x.shape
    feature_major = seq is not None
    tm = _tile(seq[2] if feature_major else m, 512)
    row = lambda n: pl.BlockSpec((tm, n), lambda i: (i, 0))
    full = _resident
    outs = [(1536, F32), (512, F32), (128, F32), (256, BF16), (512, BF16), (128, BF16),
            (256, BF16), (512, BF16)]
    out_shape = [jax.ShapeDtypeStruct((m, n), dt) for n, dt in outs]
    out_specs = [row(n) for n, _ in outs]
    ins = [x, g, w, cos, sin, gq, wuq, wuk, gkv]
    in_specs = [row(d), full(g), full(w), row(128), row(128), full(gq), full(wuq),
                full(wuk), full(gkv)]
    aliases = {}
    n_alias = 0
    nt = None
    scratch = []
    if feature_major:
        layer, nb, t, stacked, conv_w = seq
        nt = t // tm
        ins.append(conv_w)
        in_specs.append(full(conv_w))
        out_shape += [jax.ShapeDtypeStruct((nb, 256, t), BF16)] * 2
        out_specs += [pl.BlockSpec((None, 256, tm), lambda i: (i // nt, 0, i % nt))] * 2
        for a in stacked:
            out_shape.append(jax.ShapeDtypeStruct(a.shape, F32))
            out_specs.append(pl.BlockSpec((None, None, a.shape[2], tm),
                                          lambda i: (layer, i // nt, 0, i % nt)))
        n_alias = len(stacked)
        aliases = {len(ins) + k: len(outs) + 2 + k for k in range(n_alias)}
        ins += list(stacked)
        in_specs += [pl.BlockSpec(memory_space=pl.ANY)] * n_alias
        out_shape.append(jax.ShapeDtypeStruct((nb, 8, 1536), F32))
        out_specs.append(pl.BlockSpec((None, 8, 1536), lambda i: (i // nt, 0, 0)))
        scratch = [pltpu.VMEM((8 + tm, 1536), F32)]
    else:
        for n in (160, 256, 256):
            out_shape.append(jax.ShapeDtypeStruct((m, n), F32))
            out_specs.append(row(n))
    kern = functools.partial(_proj_kernel, mla_qscale=mla_qscale, diff_qscale=diff_qscale,
                             n_alias=n_alias, nt=nt)
    return pl.pallas_call(
        kern,
        out_shape=out_shape,
        grid=(m // tm,),
        in_specs=in_specs,
        out_specs=out_specs,
        scratch_shapes=scratch,
        input_output_aliases=aliases,
        compiler_params=_cp(("arbitrary" if feature_major else "parallel",)),
        name="proj_in",
    )(*ins)


def _conv_silu_norm(rows, cw_ref):
    y = rows(0) * cw_ref[0:1, :]
    for i in range(1, CONV_W):
        y = y + rows(i) * cw_ref[i:i + 1, :]
    act = _silu(y)
    parts = []
    for j in range(12):
        x = act[:, j * 128:(j + 1) * 128]
        if j < 8:
            x = x * lax.rsqrt(jnp.sum(x * x, axis=-1, keepdims=True) + EPS)
        if j < 4:
            x = x * (128.0 ** -0.5)
        parts.append(x)
    return parts


def _gdn_kernel(x_ref, z_ref, ba_ref, conv0_ref, s0_ref, cw_ref, alog_ref, dtb_ref, gout_ref,
                *refs, nbb, t_in, t_cmp, t_valid, pre, chunk):
    o_ref, sout_ref, xs_scr, s_scr = refs[-4:]
    _gdn_body(x_ref, z_ref, ba_ref, conv0_ref, s0_ref, cw_ref, alog_ref, dtb_ref, gout_ref,
              o_ref, sout_ref, xs_scr, s_scr, nbb=nbb, t_in=t_in, t_cmp=t_cmp, t_valid=t_valid,
              pre=pre, chunk=chunk)


def _gdn_body(x_ref, z_ref, ba_ref, conv0_ref, s0_ref, cw_ref, alog_ref, dtb_ref, gout_ref,
              o_ref, sout_ref, xs_scr, s_scr, *, nbb, t_in, t_cmp, t_valid, pre, chunk):
    C = chunk
    nc = t_cmp // C
    tb = pl.program_id(1)

    @pl.when(tb == 0)
    def _():
        s_scr[...] = s0_ref[...]
        if not pre:
            for bb in range(nbb):
                xs_scr[bb, 5:8, :] = conv0_ref[bb]

    if not pre:
        @pl.when(tb != 0)
        def _():
            for bb in range(nbb):
                xs_scr[bb, 5:8, :] = xs_scr[bb, 5 + t_in:8 + t_in, :]

        for bb in range(nbb):
            xs_scr[bb, 8:8 + t_in, :] = x_ref[bb * t_in:(bb + 1) * t_in, :]
            if t_cmp > t_in:
                xs_scr[bb, 8 + t_in:8 + t_cmp, :] = jnp.zeros((t_cmp - t_in, xs_scr.shape[2]),
                                                              F32)

    ri = lax.broadcasted_iota(jnp.int32, (C, C), 0)
    ci = lax.broadcasted_iota(jnp.int32, (C, C), 1)
    incl = ri >= ci
    strict = ri > ci
    tril = jnp.where(incl, 1.0, 0.0).astype(F32)
    alog = alog_ref[...]
    dtb = dtb_ref[...]
    gout = gout_ref[...]

    trip = []
    for bb in range(nbb):
        for c in range(nc):
            r0 = c * C
            if pre:
                parts = [x_ref[bb * t_in + r0:bb * t_in + r0 + C, j * 128:(j + 1) * 128]
                         for j in range(12)]
            else:
                parts = _conv_silu_norm(
                    lambda i: xs_scr[bb, 5 + r0 + i:5 + r0 + i + C, :], cw_ref)
            if t_in >= t_cmp:
                ba = ba_ref[bb * t_in + r0:bb * t_in + r0 + C, :]
            else:
                ba = jnp.concatenate([ba_ref[bb * t_in:(bb + 1) * t_in, :],
                                      jnp.zeros((t_cmp - t_in, LANES), F32)], axis=0)
            bet_all = 1.0 / (1.0 + jnp.exp(-ba))
            xa = ba + dtb
            g_all = -jnp.exp(alog) * (jnp.maximum(xa, 0.0)
                                      + jnp.log(1.0 + jnp.exp(-jnp.abs(xa))))
            if t_valid < t_cmp:
                valid = lax.broadcasted_iota(jnp.int32, (C, 1), 0) + r0 < t_valid
                parts = [jnp.where(valid, x, 0.0) for x in parts]
                bet_all = jnp.where(valid, bet_all, 0.0)
                g_all = jnp.where(valid, g_all, 0.0)
            gam_all = jnp.dot(tril, g_all, preferred_element_type=F32,
                              precision=lax.Precision.HIGHEST)
            gam_t = gam_all.T
            for h in range(4):
                q, k, v = parts[h], parts[4 + h], parts[8 + h]
                bet = bet_all[:, h:h + 1]
                gcol = gam_all[:, 4 + h:5 + h]
                grow = gam_t[4 + h:5 + h, :]
                glast = gam_all[C - 1:C, 4 + h:5 + h]
                diff = gcol - grow
                decay = jnp.where(incl, jnp.exp(jnp.where(incl, diff, 0.0)), 0.0)
                eg = jnp.exp(gcol)
                trip.append(dict(bb=bb, c=c, h=h, q=q, k=k, v=v, bet=bet, decay=decay, eg=eg,
                                 kd=k * jnp.exp(glast - gcol), sdec=jnp.exp(glast)))

    for t in trip:
        t["kk"] = _mm_nt(t["k"], t["k"])
    for t in trip:
        t["qk"] = _mm_nt(t["q"], t["k"]) * t["decay"]
    for t in trip:
        t["tp"] = jnp.where(strict, t["bet"] * t["kk"] * t["decay"], 0.0)
        t["x"] = jnp.concatenate([t["v"] * t["bet"], t["k"] * (t["bet"] * t["eg"])], axis=-1)
    b16r, b16c = lax.shift_right_logical(ri, 4), lax.shift_right_logical(ci, 4)
    b32r, b32c = lax.shift_right_logical(ri, 5), lax.shift_right_logical(ci, 5)
    in16 = b16r == b16c
    in32 = b32r == b32c
    eye = jnp.where(ri == ci, 1.0, 0.0).astype(F32)
    for t in trip:
        t["td"] = jnp.where(in16, t["tp"], 0.0)
    for t in trip:
        t["s2"] = _mm(t["td"], t["td"])
    for t in trip:
        t["s4"] = _mm(t["s2"], t["s2"])
        t["a"] = eye - t["td"] + t["s2"] - _mm(t["td"], t["s2"])
    for t in trip:
        t["s8"] = _mm(t["s4"], t["s4"])
    for t in trip:
        t["bm"] = t["s4"] + t["s8"] + _mm(t["s4"], t["s8"])
    for t in trip:
        t["p"] = t["a"] + _mm(t["a"], t["bm"])
    if C >= 32:
        for t in trip:
            t["lp"] = _mm(jnp.where(in32 & ~in16, t["tp"], 0.0), t["p"])
        for t in trip:
            t["p"] = t["p"] - _mm(t["p"], t["lp"])
    if C >= 64:
        for t in trip:
            t["lp"] = _mm(jnp.where(in32, 0.0, t["tp"]), t["p"])
        for t in trip:
            t["p"] = t["p"] - _mm(t["p"], t["lp"])
    for t in trip:
        t["x"] = _mm(t["p"], t["x"])
    state = {(bb, h): s_scr[bb, h] for bb in range(nbb) for h in range(4)}
    for c in range(nc):
        cur = [t for t in trip if t["c"] == c]
        for t in cur:
            s_h = state[(t["bb"], t["h"])]
            t["ws"] = _mm(t["x"][:, 128:256], s_h)
            t["qs"] = _mm(t["q"], s_h)
        for t in cur:
            t["u"] = t["x"][:, 0:128] - t["ws"]
        for t in cur:
            t["o"] = t["eg"] * t["qs"] + _mm(t["qk"], t["u"])
            key = (t["bb"], t["h"])
            state[key] = t["sdec"] * state[key] + _mm_tn(t["kd"], t["u"])
        r0 = c * C
        if r0 < t_in:
            n = min(C, t_in - r0)
            for t in cur:
                bb, h = t["bb"], t["h"]
                zz = z_ref[bb * t_in + r0:bb * t_in + r0 + n, h * 128:(h + 1) * 128]
                o_ref[bb * t_in + r0:bb * t_in + r0 + n, h * 128:(h + 1) * 128] = (
                    _rms(t["o"][0:n, :], gout) * _silu(zz)).astype(o_ref.dtype)
    for (bb, h), s_h in state.items():
        s_scr[bb, h] = s_h

    @pl.when(tb == pl.num_programs(1) - 1)
    def _():
        sout_ref[...] = s_scr[...]


def _gdn(qkv, z, ba, conv0, s0, layer, cw, alog, dtb, gout, s_stack, *, nb, t, out_layer, pre):
    if pre:
        chunk = DN_CHUNK
        assert t % chunk == 0
        nbb = 1
        t_in = t_cmp = t_valid = _tile(t, 512)
    else:
        assert t <= DN_CHUNK
        chunk = next(c for c in (16, 32, DN_CHUNK) if c >= t)
        nbb = _tile(nb, 8)
        t_in, t_cmp, t_valid = t, chunk, t
    nt = t // t_in
    rows = lambda n: pl.BlockSpec((nbb * t_in, n), lambda b, j: (b * nt + j, 0))
    full = lambda a: pl.BlockSpec(a.shape, lambda b, j: (0,) * a.ndim)
    kern = functools.partial(_gdn_kernel, nbb=nbb, t_in=t_in, t_cmp=t_cmp, t_valid=t_valid,
                             pre=pre, chunk=chunk)
    ins = [qkv, z, ba, conv0, s0, cw, alog, dtb, gout, s_stack]
    in_specs = [rows(1536), rows(512), rows(128),
                pl.BlockSpec((None, nbb, 3, 1536), lambda b, j: (layer, b, 0, 0)),
                pl.BlockSpec((None, nbb, 4, 128, 128), lambda b, j: (layer, b, 0, 0, 0)),
                full(cw), full(alog), full(dtb), full(gout), pl.BlockSpec(memory_space=pl.ANY)]
    return pl.pallas_call(
        kern,
        out_shape=[jax.ShapeDtypeStruct((nb * t, 512), BF16 if t_in % 16 == 0 else F32),
                   jax.ShapeDtypeStruct(s_stack.shape, F32)],
        grid=(nb // nbb, nt),
        in_specs=in_specs,
        out_specs=[rows(512), pl.BlockSpec((None, nbb, 4, 128, 128),
                                           lambda b, j: (out_layer, b, 0, 0, 0))],
        scratch_shapes=[pltpu.VMEM((nbb, 8 + t_cmp, 1536), F32),
                        pltpu.VMEM((nbb, 4, 128, 128), F32)],
        input_output_aliases={len(ins) - 1: 1},
        compiler_params=_cp(("parallel", "arbitrary")),
        name="gdn",
    )(*ins)


def _causal_sweep(i, tq, tk, nheads, step):
    assert tk in (tq, 2 * tq)
    n_big = (i * tq) // tk

    def big(j, carry):
        step(pl.multiple_of(j * tk, tk), tk, None)
        return carry

    lax.fori_loop(0, n_big, big, 0)
    if tk != tq:
        @pl.when(i * tq != n_big * tk)
        def _():
            step(pl.multiple_of(n_big * tk, tq), tq, None)

    mask = (lax.broadcasted_iota(jnp.int32, (tq, tq), 1)
            <= lax.broadcasted_iota(jnp.int32, (tq, tq), 0))
    step(pl.multiple_of(i * tq, tq), tq, jnp.concatenate([mask] * nheads, axis=0))


def _online_softmax(s, m_scr, tk):
    m_prev = m_scr[...]
    m_next = jnp.maximum(m_prev, jnp.max(s, axis=-1, keepdims=True))
    p = jnp.exp2(s - _rep(m_next, tk // LANES)).astype(BF16)
    alpha = jnp.exp2(m_prev - m_next)
    m_scr[...] = m_next
    return p, alpha


def _mla_attn_kernel(qlat_ref, qrope_ref, kp_ref, kpt_ref, wuv_ref, o_ref, q_scr, m_scr, acc_scr,
                     *, tq, tk):
    i = pl.program_id(1)
    qr = qrope_ref[...]
    lane = lax.broadcasted_iota(jnp.int32, qr.shape, 1)
    for h in range(4):
        q_scr[h * tq:(h + 1) * tq, 0:128] = qlat_ref[:, h * 128:(h + 1) * 128]
        q_scr[h * tq:(h + 1) * tq, 128:256] = jnp.where(_lane_group(lane, 32, h), qr,
                                                        jnp.zeros_like(qr))
    m_scr[...] = jnp.full(m_scr.shape, -jnp.inf, F32)
    acc_scr[...] = jnp.zeros(acc_scr.shape, F32)

    def step(start, width, mask):
        s = jnp.dot(q_scr[...], kpt_ref[:, pl.ds(start, width)], preferred_element_type=F32)
        if mask is not None:
            s = jnp.where(mask, s, NEG)
        p, alpha = _online_softmax(s, m_scr, width)
        vext = jnp.concatenate([kp_ref[pl.ds(start, width), 0:128],
                                jnp.ones((width, LANES), BF16)], axis=1)
        acc_scr[...] = acc_scr[...] * _rep(alpha, 2) + jnp.dot(
            p, vext, preferred_element_type=F32)

    _causal_sweep(i, tq, tk, 4, step)
    o = (acc_scr[:, 0:128] / acc_scr[:, 128:256]).astype(BF16)
    bo = jnp.dot(o[0:tq], wuv_ref[0:128, :], preferred_element_type=F32)
    for h in range(1, 4):
        bo = bo + jnp.dot(o[h * tq:(h + 1) * tq], wuv_ref[h * 128:(h + 1) * 128, :],
                          preferred_element_type=F32)
    o_ref[...] = bo.astype(BF16)


def _mla_attn(qlat, qrope, kp, kpt, wuv, *, nb, t):
    tq = _tile(t, 512)
    tk = _tile(t, 512)
    nq = t // tq
    kern = functools.partial(_mla_attn_kernel, tq=tq, tk=tk)
    return pl.pallas_call(
        kern,
        out_shape=jax.ShapeDtypeStruct((nb * t, 256), BF16),
        grid=(nb, nq),
        in_specs=[pl.BlockSpec((tq, 512), lambda b, i: (b * nq + i, 0)),
                  pl.BlockSpec((tq, 128), lambda b, i: (b * nq + i, 0)),
                  pl.BlockSpec((t, 256), lambda b, i: (b, 0)),
                  pl.BlockSpec((None, 256, t), lambda b, i: (b, 0, 0)),
                  pl.BlockSpec(wuv.shape, lambda b, i: (0, 0))],
        out_specs=pl.BlockSpec((tq, 256), lambda b, i: (b * nq + i, 0)),
        scratch_shapes=[pltpu.VMEM((4 * tq, 256), BF16), pltpu.VMEM((4 * tq, LANES), F32),
                        pltpu.VMEM((4 * tq, 256), F32)],
        compiler_params=_cp(("parallel", "parallel")),
        name="mla_attn",
    )(qlat, qrope, kp, kpt, wuv)


def _diff_lambda(lq1, lk1, lq2, lk2, lam_init):
    return (jnp.exp(jnp.sum(lq1 * lk1, axis=-1, keepdims=True))
            - jnp.exp(jnp.sum(lq2 * lk2, axis=-1, keepdims=True)) + lam_init)


def _diff_attn_kernel(q_ref, kt_ref, v_ref, lq1_ref, lk1_ref, lq2_ref, lk2_ref, gsub_ref, o_ref,
                      q_scr, m_scr, acc_scr, *, tq, tk, lam_init):
    i = pl.program_id(1)
    q = q_ref[...]
    lane = lax.broadcasted_iota(jnp.int32, q.shape, 1)
    for h in range(8):
        q_scr[h * tq:(h + 1) * tq, :] = jnp.where(_lane_group(lane, 32, h), q, jnp.zeros_like(q))
    m_scr[...] = jnp.full(m_scr.shape, -jnp.inf, F32)
    acc_scr[...] = jnp.zeros(acc_scr.shape, F32)

    def step(start, width, mask):
        s = jnp.dot(q_scr[...], kt_ref[:, pl.ds(start, width)], preferred_element_type=F32)
        if mask is not None:
            s = jnp.where(mask, s, NEG)
        p, alpha = _online_softmax(s, m_scr, width)
        alpha = _rep(alpha, 2)
        ones = jnp.ones((width, LANES), BF16)
        for c in range(4):
            vext = jnp.concatenate([v_ref[pl.ds(start, width), c * 128:(c + 1) * 128], ones],
                                   axis=1)
            r = slice(2 * c * tq, (2 * c + 2) * tq)
            acc_scr[r, :] = acc_scr[r, :] * alpha[r, :] + jnp.dot(
                p[r, :], vext, preferred_element_type=F32)

    _causal_sweep(i, tq, tk, 8, step)
    lam = _diff_lambda(lq1_ref[...], lk1_ref[...], lq2_ref[...], lk2_ref[...], lam_init)
    on = acc_scr[:, 0:128] / acc_scr[:, 128:256]
    lane128 = lax.broadcasted_iota(jnp.int32, (tq, LANES), 1)
    outs = []
    for c in range(4):
        d = on[2 * c * tq:(2 * c + 1) * tq] - lam * on[(2 * c + 1) * tq:(2 * c + 2) * tq]
        outs.append(_rms(d, gsub_ref[...]) * (1.0 - lam_init))
    for j in range(2):
        o_ref[:, j * 128:(j + 1) * 128] = jnp.where(
            lane128 < 64, outs[2 * j], outs[2 * j + 1]).astype(BF16)


def _diff_attn(q, kt, vdup, lq1, lk1, lq2, lk2, gsub, *, nb, t, lam_init):
    tq = _tile(t, 256)
    tk = _tile(t, 512)
    nq = t // tq
    kern = functools.partial(_diff_attn_kernel, tq=tq, tk=tk, lam_init=lam_init)
    vec = lambda a: pl.BlockSpec(a.shape, lambda b, i: (0, 0))
    return pl.pallas_call(
        kern,
        out_shape=jax.ShapeDtypeStruct((nb * t, 256), BF16),
        grid=(nb, nq),
        in_specs=[pl.BlockSpec((tq, 256), lambda b, i: (b * nq + i, 0)),
                  pl.BlockSpec((None, 256, t), lambda b, i: (b, 0, 0)),
                  pl.BlockSpec((t, 512), lambda b, i: (b, 0)),
                  vec(lq1), vec(lk1), vec(lq2), vec(lk2), vec(gsub)],
        out_specs=pl.BlockSpec((tq, 256), lambda b, i: (b * nq + i, 0)),
        scratch_shapes=[pltpu.VMEM((8 * tq, 256), BF16), pltpu.VMEM((8 * tq, LANES), F32),
                        pltpu.VMEM((8 * tq, 256), F32)],
        compiler_params=_cp(("parallel", "parallel")),
        name="diff_attn",
    )(q, kt, vdup, lq1, lk1, lq2, lk2, gsub)


def _paged_kernel(pt_ref, mla_hbm, dk_hbm, dv_hbm, qlat_ref, qr_ref, dq_ref, row_ref, dkn_ref,
                  dvn_ref, wuv_ref, lq1_ref, lk1_ref, lq2_ref, lk2_ref, gsub_ref, bo_ref, co_ref,
                  mla_buf, dk_buf, dv_buf, sem, dqb_scr, ma_scr, la_scr, aa_scr, md_scr, ld_scr,
                  ad_scr, *, g, nslot, ts, layer, lam_init):
    b = pl.program_id(0)
    nb = pl.num_programs(0)

    def copies(seq, slot):
        out = []
        for gi in range(g):
            pg = pt_ref[seq, slot * g + gi]
            for hbm, buf in ((mla_hbm, mla_buf), (dk_hbm, dk_buf), (dv_hbm, dv_buf)):
                out.append(pltpu.make_async_copy(hbm.at[layer, pg], buf.at[slot, gi],
                                                 sem.at[slot]))
        return out

    last = b == nb - 1
    nxt_b = jnp.where(last, b, b + 1)

    @pl.when(b == 0)
    def _():
        for k in range(nslot):
            for c in copies(b, k):
                c.start()

    q = dq_ref[...]
    lane = lax.broadcasted_iota(jnp.int32, q.shape, 1)
    for h in range(8):
        dqb_scr[h * ts:(h + 1) * ts, :] = jnp.where(_lane_group(lane, 32, h), q, 0.0)
    ma_scr[...] = jnp.full(ma_scr.shape, -jnp.inf, F32)
    la_scr[...] = jnp.zeros(la_scr.shape, F32)
    aa_scr[...] = jnp.zeros(aa_scr.shape, F32)
    md_scr[...] = jnp.full(md_scr.shape, -jnp.inf, F32)
    ld_scr[...] = jnp.zeros(ld_scr.shape, F32)
    ad_scr[...] = jnp.zeros(ad_scr.shape, F32)

    qlat = qlat_ref[...].astype(BF16)
    qr = qr_ref[...].astype(BF16)
    dqb = dqb_scr[...].astype(BF16)

    def weights(s_list, m_scr, l_scr):
        s = jnp.concatenate(s_list, axis=-1) if len(s_list) > 1 else s_list[0]
        m_old = m_scr[...]
        m_new = jnp.maximum(m_old, jnp.max(s, axis=-1, keepdims=True))
        p = jnp.exp2(s - m_new)
        corr = jnp.exp2(m_old - m_new)
        l_scr[...] = l_scr[...] * corr + jnp.sum(p, axis=-1, keepdims=True)
        m_scr[...] = m_new
        return p.astype(BF16), corr

    def accumulate(pb, corr, widths, pv, acc_scr):
        acc = acc_scr[...] * corr
        off = 0
        for idx, n in enumerate(widths):
            acc = acc + pv(pb[:, off:off + n], idx)
            off += n
        acc_scr[...] = acc

    def update(s_list, pv, m_scr, l_scr, acc_scr):
        pb, corr = weights(s_list, m_scr, l_scr)
        accumulate(pb, corr, [sb.shape[1] for sb in s_list], pv, acc_scr)

    def consume(slot):
        kts = [mla_buf[slot, gi].astype(BF16) for gi in range(g)]
        sa = [jnp.dot(qlat, kt[0:128, :], preferred_element_type=F32)
              + jnp.dot(qr, kt[128:160, :], preferred_element_type=F32) for kt in kts]
        sd = [jnp.dot(dqb, dk_buf[slot, gi].astype(BF16), preferred_element_type=F32)
              for gi in range(g)]
        pa, ca = weights(sa, ma_scr, la_scr)
        pd, cd = weights(sd, md_scr, ld_scr)
        accumulate(pa, ca, [PAGE] * g, lambda pb, idx: _mm_nt(pb, kts[idx][0:128, :]), aa_scr)
        accumulate(pd, cd, [PAGE] * g, lambda pb, idx: _mm_nt(pb, dv_buf[slot, idx]), ad_scr)

    for k in range(nslot):
        for c in copies(b, k):
            c.wait()
        consume(k)
        for c in copies(nxt_b, k):
            c.start()

    @pl.when(last)
    def _():
        for k in range(nslot):
            for c in copies(nxt_b, k):
                c.wait()

    def finalize():
        tsp = -(-ts // 16) * 16
        qpos = lax.broadcasted_iota(jnp.int32, (ts, tsp), 0)
        kpos = lax.broadcasted_iota(jnp.int32, (ts, tsp), 1)
        mask = kpos <= qpos

        def padded(ref):
            rows = ref[...]
            return jnp.concatenate([rows, jnp.zeros((tsp - ts, rows.shape[1]), F32)],
                                   axis=0).astype(BF16)

        rows = padded(row_ref)
        s = _mm_nt(qlat, rows[:, 0:128]) + _mm_nt(qr, rows[:, 128:160])
        update([jnp.where(jnp.concatenate([mask] * 4, axis=0), s, NEG)],
               lambda pb, idx: jnp.dot(pb, rows[:, 0:128], preferred_element_type=F32),
               ma_scr, la_scr, aa_scr)
        s = _mm_nt(dqb, padded(dkn_ref))
        vn = padded(dvn_ref)
        update([jnp.where(jnp.concatenate([mask] * 8, axis=0), s, NEG)],
               lambda pb, idx: jnp.dot(pb, vn, preferred_element_type=F32),
               md_scr, ld_scr, ad_scr)
        o = (aa_scr[...] / la_scr[...]).astype(BF16)
        bo = jnp.zeros((ts, 256), F32)
        for h in range(4):
            full = jnp.dot(o, wuv_ref[h * 128:(h + 1) * 128, :], preferred_element_type=F32)
            bo = bo + full[h * ts:(h + 1) * ts]
        bo_ref[...] = bo
        on = ad_scr[...] / ld_scr[...]
        lam = _diff_lambda(lq1_ref[...], lk1_ref[...], lq2_ref[...], lk2_ref[...], lam_init)
        lane = lax.broadcasted_iota(jnp.int32, (ts, 256), 1)
        d = jnp.zeros((ts, 256), F32)
        for c in range(4):
            dc = on[2 * c * ts:(2 * c + 1) * ts] - lam * on[(2 * c + 1) * ts:(2 * c + 2) * ts]
            d = jnp.where(_lane_group(lane, 64, c), dc, d)
        ms = jnp.zeros((ts, 256), F32)
        for c in range(4):
            sel = _lane_group(lane, 64, c)
            ssq = jnp.sum(jnp.where(sel, d * d, 0.0), axis=-1, keepdims=True) * (1.0 / 64)
            ms = jnp.where(sel, ssq, ms)
        co_ref[...] = d * lax.rsqrt(ms + EPS) * gsub_ref[...] * (1.0 - lam_init)

    finalize()


def _paged(page_table, cache_mla_t, cache_dk_t, cache_dv_t, layer, qlat, qr, dq, row, dkn, dvn,
           wuv, lq1, lk1, lq2, lk2, gsub4, *, nb, ts, lam_init):
    n_pages = page_table.shape[1]
    g = _tile(n_pages, 16)
    nslot = n_pages // g
    kern = functools.partial(_paged_kernel, g=g, nslot=nslot, ts=ts, layer=layer,
                             lam_init=lam_init)
    per_b = lambda r, n: pl.BlockSpec((None, r, n), lambda b, pt: (b, 0, 0))
    vec = lambda a: pl.BlockSpec(a.shape, lambda b, pt: (0, 0))
    hbm = pl.BlockSpec(memory_space=pl.ANY)
    in_specs = [hbm, hbm, hbm,
                per_b(4 * ts, 128), per_b(4 * ts, 32), per_b(ts, 256), per_b(ts, 160),
                per_b(ts, 256), per_b(ts, 256), vec(wuv), vec(lq1), vec(lk1), vec(lq2),
                vec(lk2), vec(gsub4)]
    grid_spec = pltpu.PrefetchScalarGridSpec(
        num_scalar_prefetch=1,
        grid=(nb,),
        in_specs=in_specs,
        out_specs=[per_b(ts, 256), per_b(ts, 256)],
        scratch_shapes=[pltpu.VMEM((nslot, g, 160, PAGE), F32),
                        pltpu.VMEM((nslot, g, 256, PAGE), F32),
                        pltpu.VMEM((nslot, g, 256, PAGE), F32),
                        pltpu.SemaphoreType.DMA((nslot,)),
                        pltpu.VMEM((8 * ts, 256), F32),
                        pltpu.VMEM((4 * ts, 1), F32), pltpu.VMEM((4 * ts, 1), F32),
                        pltpu.VMEM((4 * ts, 128), F32),
                        pltpu.VMEM((8 * ts, 1), F32), pltpu.VMEM((8 * ts, 1), F32),
                        pltpu.VMEM((8 * ts, 256), F32)])
    return pl.pallas_call(
        kern,
        out_shape=[jax.ShapeDtypeStruct((nb, ts, 256), F32),
                   jax.ShapeDtypeStruct((nb, ts, 256), F32)],
        grid_spec=grid_spec,
        compiler_params=_cp(("arbitrary",)),
        name="paged_attn",
    )(page_table, cache_mla_t, cache_dk_t, cache_dv_t,
      qlat, qr, dq, row, dkn, dvn, wuv, lq1, lk1, lq2, lk2, gsub4)


def _pack_w_in(w):
    d = w.shape[0]
    a_qkv, a_z, a_b, a_a, b_cq, b_ckv, b_kr, c_q, c_k, c_v = jnp.split(
        w, np.cumsum([1536, 512, 4, 4, 192, 128, 32, 256, 256]).tolist(), axis=1)
    z = lambda n: jnp.zeros((d, n), w.dtype)
    kr_sw = jnp.concatenate([-b_kr[:, 16:32], b_kr[:, 0:16]], axis=1)
    v4 = c_v.reshape(d, 4, 64)
    vdup = jnp.concatenate([v4, v4], axis=2).reshape(d, 512)
    return jnp.concatenate(
        [a_qkv, a_z, a_b, a_a, z(120), b_ckv, b_cq, z(64), jnp.tile(b_kr, (1, 4)),
         jnp.tile(kr_sw, (1, 4)), c_q, c_k, vdup], axis=1).astype(BF16)


def _pack_mla(w_uq, w_uk, w_uv):
    nope = w_uq[:, :, 0:64].reshape(192, 256)
    rope = w_uq[:, :, 64:96]
    rope_sw = jnp.concatenate([-rope[:, :, 16:32], rope[:, :, 0:16]], axis=2)
    wuq = jnp.concatenate([nope, rope.reshape(192, 128), rope_sw.reshape(192, 128)], axis=1)
    wuq = jnp.concatenate([wuq, jnp.zeros((64, 512), wuq.dtype)], axis=0)
    eye = jnp.eye(4, dtype=w_uk.dtype)
    wuk_bd = jnp.einsum('rhn,hg->hngr', w_uk, eye).reshape(256, 512)
    wuv_bd = jnp.einsum('rhe,hg->hrge', w_uv, eye).reshape(512, 256)
    return wuq.astype(BF16), wuk_bd.astype(BF16), wuv_bd.astype(BF16)


def _rope_tables(pos):
    freqs = jnp.power(ROPE_THETA, -jnp.arange(16, dtype=F32) / 16)
    ang = pos[:, None] * freqs[None, :]
    return jnp.tile(jnp.cos(ang), (1, 8)), jnp.tile(jnp.sin(ang), (1, 8))


def _lane_pad(v, n=LANES, offset=0):
    v = v.reshape(1, -1).astype(F32)
    return jnp.pad(v, ((0, 0), (offset, n - offset - v.shape[1])))


def kernel(x_prompt, x_sample, cache_mla, cache_diff_k, cache_diff_v, state_dn_S, state_dn_conv, page_table, g_f1_pre, g_f1_post, w_f1_gate, w_f1_up, w_f1_down, g_mix_pre, g_mix_post, w_in, w_out, dn_conv_w, dn_A_log, dn_dt_bias, dn_g_out, mla_g_q, mla_w_uq, mla_g_kv, mla_w_uk, mla_w_uv, diff_lam_q1, diff_lam_k1, diff_lam_q2, diff_lam_k2, diff_g_subln, g_f2_pre, g_f2_post, w_f2_gate, w_f2_up, w_f2_down):
    bp, tp, d = x_prompt.shape
    bs, ts, _ = x_sample.shape
    depth = w_in.shape[0]
    n_pages = page_table.shape[1]
    past_len = n_pages * PAGE
    mla_qscale = 96.0 ** -0.5 * LOG2E
    diff_qscale = 32.0 ** -0.5 * LOG2E

    cos_p, sin_p = _rope_tables(jnp.arange(tp, dtype=F32))
    cos_p, sin_p = jnp.tile(cos_p, (bp, 1)), jnp.tile(sin_p, (bp, 1))
    cos_s, sin_s = _rope_tables(past_len + jnp.arange(ts, dtype=F32))
    cos_s, sin_s = jnp.tile(cos_s, (bs, 1)), jnp.tile(sin_s, (bs, 1))

    n_pool = cache_mla.shape[1]
    cache_mla_t = jnp.swapaxes(cache_mla, 2, 3)
    cache_dk_t = jnp.transpose(cache_diff_k, (0, 1, 3, 4, 2)).reshape(depth, n_pool, 256, PAGE)
    cache_dv_t = jnp.transpose(cache_diff_v, (0, 1, 3, 4, 2)).reshape(depth, n_pool, 256, PAGE)
    zero_conv = jnp.zeros((1, bp, CONV_W - 1, 1536), F32)
    zero_s = jnp.zeros((1, bp, 4, 128, 128), F32)

    xp = x_prompt.reshape(bp * tp, d)
    xs = x_sample.reshape(bs * ts, d)
    p_conv = []
    p_fm = [jnp.zeros((depth, bp, n, tp), F32) for n in (160, 256, 256)]
    p_s = jnp.zeros((depth, bp, 4, 128, 128), F32)
    s_s = jnp.zeros((depth, bs, 4, 128, 128), F32)
    s_out = [[] for _ in range(4)]
    row1 = lambda v: v.reshape(1, -1).astype(F32)
    for l in range(depth):
        lam_init = 0.8 - 0.6 * math.exp(-0.3 * l)
        wg1, wu1, wd1 = (w_f1_gate[l].astype(BF16), w_f1_up[l].astype(BF16),
                         w_f1_down[l].astype(BF16))
        wg2, wu2, wd2 = (w_f2_gate[l].astype(BF16), w_f2_up[l].astype(BF16),
                         w_f2_down[l].astype(BF16))
        w_inp = _pack_w_in(w_in[l])
        wuq, wuk_bd, wuv_bd = _pack_mla(mla_w_uq[l], mla_w_uk[l], mla_w_uv[l])
        w_o = w_out[l].astype(BF16)
        gq = _lane_pad(mla_g_q[l], 256)
        gkv = row1(mla_g_kv[l])
        alog = _lane_pad(dn_A_log[l], offset=4)
        dtb = _lane_pad(dn_dt_bias[l], offset=4)
        gout = row1(dn_g_out[l])
        cw = dn_conv_w[l].astype(F32)
        lq1, lk1, lq2, lk2 = (_lane_pad(diff_lam_q1[l]), _lane_pad(diff_lam_k1[l]),
                              _lane_pad(diff_lam_q2[l]), _lane_pad(diff_lam_k2[l]))
        gsub2 = jnp.tile(row1(diff_g_subln[l]), (1, 2))
        gsub4 = jnp.tile(row1(diff_g_subln[l]), (1, 4))

        def front(x, cos, sin, seq=None):
            x = _ffn(x, row1(g_f1_pre[l]), wg1, wu1, wd1, row1(g_f1_post[l]))
            return x, _proj(x, row1(g_mix_pre[l]), w_inp, cos, sin, gq, wuq, wuk_bd, gkv,
                            mla_qscale=mla_qscale, diff_qscale=diff_qscale, seq=seq)

        def back(x, ao, bo, co):
            return _ffn(x, row1(g_f2_pre[l]), wg2, wu2, wd2, row1(g_f2_post[l]),
                        mix=(ao, bo, co, w_o, row1(g_mix_post[l])))

        xp, (qkv, z, ba, kp, qlat, qrope, dq, dvb, kpt, dkbt, *p_fm, tail) = front(
            xp, cos_p, sin_p, seq=(l, bp, tp, p_fm, cw))
        ao, p_s = _gdn(qkv, z, ba, zero_conv, zero_s, 0, cw, alog, dtb, gout, p_s, nb=bp, t=tp,
                       out_layer=l, pre=True)
        bo = _mla_attn(qlat, qrope, kp, kpt, wuv_bd, nb=bp, t=tp)
        co = _diff_attn(dq, dkbt, dvb, lq1, lk1, lq2, lk2, gsub2, nb=bp, t=tp, lam_init=lam_init)
        xp = back(xp, ao, bo, co)
        p_conv.append(tail[:, 8 - (CONV_W - 1):])

        xs, (qkv, z, ba, kp, qlat, qrope, dq, dvb, mrow, dk, dv) = front(xs, cos_s, sin_s)
        ao, s_s = _gdn(qkv, z, ba, state_dn_conv, state_dn_S, l, cw, alog, dtb, gout, s_s,
                       nb=bs, t=ts, out_layer=l, pre=False)
        qlat_h = qlat.reshape(bs, ts, 4, 128).transpose(0, 2, 1, 3).reshape(bs, 4 * ts, 128)
        qr_h = qrope.reshape(bs, ts, 4, 32).transpose(0, 2, 1, 3).reshape(bs, 4 * ts, 32)
        bo, co = _paged(page_table, cache_mla_t, cache_dk_t, cache_dv_t, l, qlat_h, qr_h,
                        dq.reshape(bs, ts, 256).astype(F32), mrow.reshape(bs, ts, 160),
                        dk.reshape(bs, ts, 256), dv.reshape(bs, ts, 256), wuv_bd,
                        lq1, lk1, lq2, lk2, gsub4, nb=bs, ts=ts, lam_init=lam_init)
        xs = back(xs, ao, bo.reshape(bs * ts, 256), co.reshape(bs * ts, 256))
        s_out[0].append(qkv.reshape(bs, ts, 1536)[:, ts - (CONV_W - 1):])
        s_out[1].append(mrow.reshape(bs, ts, 160))
        s_out[2].append(dk.reshape(bs, ts, 8, 32))
        s_out[3].append(dv.reshape(bs, ts, 4, 64))

    rowt, dkt, dvt = p_fm
    p_mla = jnp.swapaxes(rowt, 2, 3)
    p_dk = jnp.transpose(dkt.reshape(depth, bp, 8, 32, tp), (0, 1, 4, 2, 3))
    p_dv = jnp.transpose(dvt.reshape(depth, bp, 4, 64, tp), (0, 1, 4, 2, 3))
    s_conv, s_mla, s_dk, s_dv = [jnp.stack(a, axis=0) for a in s_out]
    return (xp.reshape(bp, tp, d), xs.reshape(bs, ts, d), p_mla, p_dk, p_dv, p_s,
            jnp.stack(p_conv, axis=0), s_mla, s_dk, s_dv, s_s, s_conv)
```

```python
import functools
import math

import jax
import jax.numpy as jnp
import numpy as np
from jax import lax
from jax.experimental import pallas as pl
from jax.experimental.pallas import tpu as pltpu

F32 = jnp.float32
BF16 = jnp.bfloat16
EPS = 1e-6
ROPE_THETA = 10000.0
CONV_W = 4
DN_CHUNK = 64
PAGE = 128
LANES = 128
LOG2E = math.log2(math.e)
MXU_N = 256
VMEM_LIMIT = 52 * 1024 * 1024
NEG = -1e30

_P_QKV, _P_Z, _P_BA, _P_CQ, _P_KR, _P_DQ, _P_DK, _P_DV, _P_END = (
    0, 1536, 2048, 2304, 2560, 2816, 3072, 3328, 3840)


def _cp(sem):
    return pltpu.CompilerParams(dimension_semantics=sem, vmem_limit_bytes=VMEM_LIMIT)


def _tile(m, pref):
    t = pref
    while m % t:
        t //= 2
    return t


def _mm(a, b):
    return jnp.dot(a.astype(BF16), b.astype(BF16), preferred_element_type=F32)


def _mm_nt(a, b):
    return lax.dot_general(a.astype(BF16), b.astype(BF16), (((1,), (1,)), ((), ())),
                           preferred_element_type=F32)


def _mm_tn(a, b):
    return lax.dot_general(a.astype(BF16), b.astype(BF16), (((0,), (0,)), ((), ())),
                           preferred_element_type=F32)


def _rms(x, g, n=None):
    n = x.shape[-1] if n is None else n
    ms = jnp.sum(x * x, axis=-1, keepdims=True) * (1.0 / n)
    return x * lax.rsqrt(ms + EPS) * g


def _silu(x):
    return x / (1.0 + jnp.exp(-x))


def _lane_group(lane, width, idx):
    return (lane >= idx * width) & (lane < (idx + 1) * width)


def _rep(x, k):
    return x if k == 1 else jnp.concatenate([x] * k, axis=1)


def _ffn_body(x, gpre_ref, wg_ref, wu_ref, wd_ref, gpost_ref, act_scr):
    h = _rms(x, gpre_ref[...]).astype(BF16)
    ff = wg_ref.shape[1]
    cw = MXU_N
    for c in range(ff // cw):
        a = jnp.dot(h, wg_ref[:, c * cw:(c + 1) * cw], preferred_element_type=F32)
        b = jnp.dot(h, wu_ref[:, c * cw:(c + 1) * cw], preferred_element_type=F32)
        act_scr[:, c * cw:(c + 1) * cw] = (_silu(a) * b).astype(BF16)
    y = jnp.dot(act_scr[...], wd_ref[...], preferred_element_type=F32)
    return x + 0.5 * _rms(y, gpost_ref[...])


def _ffn_kernel(x_ref, gpre_ref, wg_ref, wu_ref, wd_ref, gpost_ref, o_ref, act_scr):
    o_ref[...] = _ffn_body(x_ref[...], gpre_ref, wg_ref, wu_ref, wd_ref, gpost_ref, act_scr)


def _outproj_ffn_kernel(x_ref, ao_ref, bo_ref, co_ref, wo_ref, gmix_ref, gpre_ref, wg_ref, wu_ref,
                        wd_ref, gpost_ref, o_ref, act_scr):
    y = (_mm(ao_ref[...], wo_ref[0:512, :]) + _mm(bo_ref[...], wo_ref[512:768, :])
         + _mm(co_ref[...], wo_ref[768:1024, :]))
    x = x_ref[...] + _rms(y, gmix_ref[...])
    o_ref[...] = _ffn_body(x, gpre_ref, wg_ref, wu_ref, wd_ref, gpost_ref, act_scr)


def _resident(a):
    return pl.BlockSpec(a.shape, lambda i: (0,) * a.ndim, pipeline_mode=pl.Buffered(1))


def _ffn(x, gpre, wg, wu, wd, gpost, mix=None):
    m, d = x.shape
    ff = wg.shape[1]
    assert ff % MXU_N == 0
    tm = _tile(m, 512)
    row = lambda n: pl.BlockSpec((tm, n), lambda i: (i, 0))
    weights = [gpre, wg, wu, wd, gpost]
    if mix is None:
        kern, ins, in_specs = _ffn_kernel, [x], [row(d)]
    else:
        ao, bo, co, wo, gmix = mix
        kern, ins = _outproj_ffn_kernel, [x, ao, bo, co, wo, gmix]
        in_specs = [row(d), row(512), row(256), row(256), _resident(wo), _resident(gmix)]
    return pl.pallas_call(
        kern,
        out_shape=jax.ShapeDtypeStruct((m, d), F32),
        grid=(m // tm,),
        in_specs=in_specs + [_resident(w) for w in weights],
        out_specs=row(d),
        scratch_shapes=[pltpu.VMEM((tm, ff), BF16)],
        compiler_params=_cp(("parallel",)),
        name="ffn" if mix is None else "outproj_ffn",
    )(*ins, *weights)


def _proj_kernel(x_ref, g_ref, w_ref, cos_ref, sin_ref, gq_ref, wuq_ref, wuk_ref, gkv_ref,
                 *refs, mla_qscale, diff_qscale, n_alias, nt):
    fresh = nt is not None
    if fresh:
        cw_ref = refs[0]
        refs = refs[1 + n_alias:]
    qkv_ref, z_ref, ba_ref, kp_ref, qlat_ref, qrope_ref, dq_ref, dvb_ref = refs[0:8]
    extra = refs[8:]
    h = _rms(x_ref[...], g_ref[...]).astype(BF16)

    def proj(lo, hi):
        return jnp.dot(h, w_ref[:, lo:hi], preferred_element_type=F32)

    if fresh:
        tail_ref, xs_scr = extra[5:]
        tm = x_ref.shape[0]

        @pl.when(lax.rem(pl.program_id(0), nt) == 0)
        def _():
            xs_scr[0:8, :] = jnp.zeros((8, xs_scr.shape[1]), F32)

        xs_scr[8:8 + tm, :] = proj(_P_QKV, _P_Z)
        rc = min(tm, 128)
        for r0 in range(0, tm, rc):
            parts = _conv_silu_norm(lambda i: xs_scr[5 + r0 + i:5 + r0 + i + rc, :], cw_ref)
            for j, part in enumerate(parts):
                qkv_ref[r0:r0 + rc, j * 128:(j + 1) * 128] = part
        tail = xs_scr[tm:tm + 8, :]
        tail_ref[...] = tail
        xs_scr[0:8, :] = tail
    else:
        qkv_ref[...] = proj(_P_QKV, _P_Z)
    z_ref[...] = proj(_P_Z, _P_BA)
    ba_ckv = proj(_P_BA, _P_CQ)
    ba_ref[...] = ba_ckv[:, 0:128]
    cos = cos_ref[...]
    sin = sin_ref[...]
    cq = _rms(proj(_P_CQ, _P_KR), gq_ref[...], n=192)
    qb = _mm(cq, wuq_ref[...])
    qlat_ref[...] = (_mm(qb[:, 0:256], wuk_ref[...]) * mla_qscale).astype(BF16)
    qrope_ref[...] = ((qb[:, 256:384] * cos + qb[:, 384:512] * sin) * mla_qscale).astype(BF16)
    ckv = _rms(ba_ckv[:, 128:256], gkv_ref[...])
    kr2 = proj(_P_KR, _P_DQ)
    kr = kr2[:, 0:128] * cos + kr2[:, 128:256] * sin
    kp_ref[:, 0:128] = ckv.astype(BF16)
    kp_ref[:, 128:256] = kr.astype(BF16)
    dq_ref[...] = (proj(_P_DQ, _P_DK) * diff_qscale).astype(BF16)
    dk = proj(_P_DK, _P_DV)
    dvd = proj(_P_DV, _P_END)
    dvb_ref[...] = dvd.astype(BF16)
    lane = lax.broadcasted_iota(jnp.int32, (dvd.shape[0], LANES), 1)
    dv = [jnp.where(lane < 64, dvd[:, (2 * j) * 128:(2 * j + 1) * 128],
                    dvd[:, (2 * j + 1) * 128:(2 * j + 2) * 128]) for j in range(2)]
    if fresh:
        kpt_ref, dkbt_ref, rowt_ref, dkt_ref, dvt_ref = extra[0:5]
        ckv_t, kr_t, dk_t = ckv.T, kr.T, dk.T
        kpt_ref[0:128, :] = ckv_t.astype(BF16)
        kpt_ref[128:256, :] = kr_t.astype(BF16)
        dkbt_ref[...] = dk_t.astype(BF16)
        rowt_ref[0:128, :] = ckv_t
        rowt_ref[128:160, :] = kr_t[0:32, :]
        dkt_ref[...] = dk_t
        for j in range(2):
            dvt_ref[j * 128:(j + 1) * 128, :] = dv[j].T
    else:
        row_ref, dk_ref, dv_ref = extra
        row_ref[:, 0:128] = ckv
        row_ref[:, 128:160] = kr[:, 0:32]
        dk_ref[...] = dk
        for j in range(2):
            dv_ref[:, j * 128:(j + 1) * 128] = dv[j]


def _proj(x, g, w, cos, sin, gq, wuq, wuk, gkv, *, mla_qscale, diff_qscale, seq=None):
    m, d = x.shape
    feature_major = seq is not None
    tm = _tile(seq[2] if feature_major else m, 512)
    row = lambda n: pl.BlockSpec((tm, n), lambda i: (i, 0))
    full = _resident
    outs = [(1536, F32), (512, F32), (128, F32), (256, BF16), (512, BF16), (128, BF16),
            (256, BF16), (512, BF16)]
    out_shape = [jax.ShapeDtypeStruct((m, n), dt) for n, dt in outs]
    out_specs = [row(n) for n, _ in outs]
    ins = [x, g, w, cos, sin, gq, wuq, wuk, gkv]
    in_specs = [row(d), full(g), full(w), row(128), row(128), full(gq), full(wuq),
                full(wuk), full(gkv)]
    aliases = {}
    n_alias = 0
    nt = None
    scratch = []
    if feature_major:
        layer, nb, t, stacked, conv_w = seq
        nt = t // tm
        ins.append(conv_w)
        in_specs.append(full(conv_w))
        out_shape += [jax.ShapeDtypeStruct((nb, 256, t), BF16)] * 2
        out_specs += [pl.BlockSpec((None, 256, tm), lambda i: (i // nt, 0, i % nt))] * 2
        for a in stacked:
            out_shape.append(jax.ShapeDtypeStruct(a.shape, F32))
            out_specs.append(pl.BlockSpec((None, None, a.shape[2], tm),
                                          lambda i: (layer, i // nt, 0, i % nt)))
        n_alias = len(stacked)
        aliases = {len(ins) + k: len(outs) + 2 + k for k in range(n_alias)}
        ins += list(stacked)
        in_specs += [pl.BlockSpec(memory_space=pl.ANY)] * n_alias
        out_shape.append(jax.ShapeDtypeStruct((nb, 8, 1536), F32))
        out_specs.append(pl.BlockSpec((None, 8, 1536), lambda i: (i // nt, 0, 0)))
        scratch = [pltpu.VMEM((8 + tm, 1536), F32)]
    else:
        for n in (160, 256, 256):
            out_shape.append(jax.ShapeDtypeStruct((m, n), F32))
            out_specs.append(row(n))
    kern = functools.partial(_proj_kernel, mla_qscale=mla_qscale, diff_qscale=diff_qscale,
                             n_alias=n_alias, nt=nt)
    return pl.pallas_call(
        kern,
        out_shape=out_shape,
        grid=(m // tm,),
        in_specs=in_specs,
        out_specs=out_specs,
        scratch_shapes=scratch,
        input_output_aliases=aliases,
        compiler_params=_cp(("arbitrary" if feature_major else "parallel",)),
        name="proj_in",
    )(*ins)


def _conv_silu_norm(rows, cw_ref):
    y = rows(0) * cw_ref[0:1, :]
    for i in range(1, CONV_W):
        y = y + rows(i) * cw_ref[i:i + 1, :]
    act = _silu(y)
    parts = []
    for j in range(12):
        x = act[:, j * 128:(j + 1) * 128]
        if j < 8:
            x = x * lax.rsqrt(jnp.sum(x * x, axis=-1, keepdims=True) + EPS)
        if j < 4:
            x = x * (128.0 ** -0.5)
        parts.append(x)
    return parts


def _gdn_kernel(x_ref, z_ref, ba_ref, conv0_ref, s0_ref, cw_ref, alog_ref, dtb_ref, gout_ref,
                *refs, nbb, t_in, t_cmp, t_valid, pre, chunk):
    o_ref, sout_ref, xs_scr, s_scr = refs[-4:]
    _gdn_body(x_ref, z_ref, ba_ref, conv0_ref, s0_ref, cw_ref, alog_ref, dtb_ref, gout_ref,
              o_ref, sout_ref, xs_scr, s_scr, nbb=nbb, t_in=t_in, t_cmp=t_cmp, t_valid=t_valid,
              pre=pre, chunk=chunk)


def _gdn_body(x_ref, z_ref, ba_ref, conv0_ref, s0_ref, cw_ref, alog_ref, dtb_ref, gout_ref,
              o_ref, sout_ref, xs_scr, s_scr, *, nbb, t_in, t_cmp, t_valid, pre, chunk):
    C = chunk
    nc = t_cmp // C
    tb = pl.program_id(1)

    @pl.when(tb == 0)
    def _():
        s_scr[...] = s0_ref[...]
        if not pre:
            for bb in range(nbb):
                xs_scr[bb, 5:8, :] = conv0_ref[bb]

    if not pre:
        @pl.when(tb != 0)
        def _():
            for bb in range(nbb):
                xs_scr[bb, 5:8, :] = xs_scr[bb, 5 + t_in:8 + t_in, :]

        for bb in range(nbb):
            xs_scr[bb, 8:8 + t_in, :] = x_ref[bb]
            if t_cmp > t_in:
                xs_scr[bb, 8 + t_in:8 + t_cmp, :] = jnp.zeros((t_cmp - t_in, xs_scr.shape[2]),
                                                              F32)

    ri = lax.broadcasted_iota(jnp.int32, (C, C), 0)
    ci = lax.broadcasted_iota(jnp.int32, (C, C), 1)
    incl = ri >= ci
    strict = ri > ci
    tril = jnp.where(incl, 1.0, 0.0).astype(F32)
    alog = alog_ref[...]
    dtb = dtb_ref[...]
    gout = gout_ref[...]

    trip = []
    for bb in range(nbb):
        for c in range(nc):
            r0 = c * C
            if pre:
                parts = [x_ref[bb, r0:r0 + C, j * 128:(j + 1) * 128] for j in range(12)]
            else:
                parts = _conv_silu_norm(
                    lambda i: xs_scr[bb, 5 + r0 + i:5 + r0 + i + C, :], cw_ref)
            if t_in >= t_cmp:
                ba = ba_ref[bb, r0:r0 + C, :]
            else:
                ba = jnp.concatenate([ba_ref[bb],
                                      jnp.zeros((t_cmp - t_in, LANES), F32)], axis=0)
            bet_all = 1.0 / (1.0 + jnp.exp(-ba))
            xa = ba + dtb
            g_all = -jnp.exp(alog) * (jnp.maximum(xa, 0.0)
                                      + jnp.log(1.0 + jnp.exp(-jnp.abs(xa))))
            if t_valid < t_cmp:
                valid = lax.broadcasted_iota(jnp.int32, (C, 1), 0) + r0 < t_valid
                parts = [jnp.where(valid, x, 0.0) for x in parts]
                bet_all = jnp.where(valid, bet_all, 0.0)
                g_all = jnp.where(valid, g_all, 0.0)
            gam_all = jnp.dot(tril, g_all, preferred_element_type=F32,
                              precision=lax.Precision.HIGHEST)
            gam_t = gam_all.T
            for h in range(4):
                q, k, v = parts[h], parts[4 + h], parts[8 + h]
                bet = bet_all[:, h:h + 1]
                gcol = gam_all[:, 4 + h:5 + h]
                grow = gam_t[4 + h:5 + h, :]
                glast = gam_all[C - 1:C, 4 + h:5 + h]
                diff = gcol - grow
                decay = jnp.where(incl, jnp.exp(jnp.where(incl, diff, 0.0)), 0.0)
                eg = jnp.exp(gcol)
                trip.append(dict(bb=bb, c=c, h=h, q=q, k=k, v=v, bet=bet, decay=decay, eg=eg,
                                 kd=k * jnp.exp(glast - gcol), sdec=jnp.exp(glast)))

    for t in trip:
        t["kk"] = _mm_nt(t["k"], t["k"])
    for t in trip:
        t["qk"] = _mm_nt(t["q"], t["k"]) * t["decay"]
    for t in trip:
        t["tp"] = jnp.where(strict, t["bet"] * t["kk"] * t["decay"], 0.0)
        t["x"] = jnp.concatenate([t["v"] * t["bet"], t["k"] * (t["bet"] * t["eg"])], axis=-1)
    b16r, b16c = lax.shift_right_logical(ri, 4), lax.shift_right_logical(ci, 4)
    b32r, b32c = lax.shift_right_logical(ri, 5), lax.shift_right_logical(ci, 5)
    in16 = b16r == b16c
    in32 = b32r == b32c
    eye = jnp.where(ri == ci, 1.0, 0.0).astype(F32)
    for t in trip:
        t["td"] = jnp.where(in16, t["tp"], 0.0)
    for t in trip:
        t["s2"] = _mm(t["td"], t["td"])
    for t in trip:
        t["s4"] = _mm(t["s2"], t["s2"])
        t["a"] = eye - t["td"] + t["s2"] - _mm(t["td"], t["s2"])
    for t in trip:
        t["s8"] = _mm(t["s4"], t["s4"])
    for t in trip:
        t["bm"] = t["s4"] + t["s8"] + _mm(t["s4"], t["s8"])
    for t in trip:
        t["p"] = t["a"] + _mm(t["a"], t["bm"])
    if C >= 32:
        for t in trip:
            t["lp"] = _mm(jnp.where(in32 & ~in16, t["tp"], 0.0), t["p"])
        for t in trip:
            t["p"] = t["p"] - _mm(t["p"], t["lp"])
    if C >= 64:
        for t in trip:
            t["lp"] = _mm(jnp.where(in32, 0.0, t["tp"]), t["p"])
        for t in trip:
            t["p"] = t["p"] - _mm(t["p"], t["lp"])
    for t in trip:
        t["x"] = _mm(t["p"], t["x"])
    state = {(bb, h): s_scr[bb, h] for bb in range(nbb) for h in range(4)}
    for c in range(nc):
        cur = [t for t in trip if t["c"] == c]
        for t in cur:
            s_h = state[(t["bb"], t["h"])]
            t["ws"] = _mm(t["x"][:, 128:256], s_h)
            t["qs"] = _mm(t["q"], s_h)
        for t in cur:
            t["u"] = t["x"][:, 0:128] - t["ws"]
        for t in cur:
            t["o"] = t["eg"] * t["qs"] + _mm(t["qk"], t["u"])
            key = (t["bb"], t["h"])
            state[key] = t["sdec"] * state[key] + _mm_tn(t["kd"], t["u"])
        r0 = c * C
        if r0 < t_in:
            n = min(C, t_in - r0)
            for t in cur:
                bb, h = t["bb"], t["h"]
                zz = z_ref[bb, r0:r0 + n, h * 128:(h + 1) * 128]
                o_ref[bb, r0:r0 + n, h * 128:(h + 1) * 128] = (
                    _rms(t["o"][0:n, :], gout) * _silu(zz)).astype(o_ref.dtype)
    for (bb, h), s_h in state.items():
        s_scr[bb, h] = s_h

    @pl.when(tb == pl.num_programs(1) - 1)
    def _():
        sout_ref[...] = s_scr[...]


def _gdn(qkv, z, ba, conv0, s0, layer, cw, alog, dtb, gout, s_stack, *, nb, t, out_layer, pre):
    if pre:
        chunk = DN_CHUNK
        assert t % chunk == 0
        nbb = _tile(nb, 2)
        t_in = t_cmp = t_valid = _tile(t, 256)
    else:
        assert t <= DN_CHUNK
        chunk = next(c for c in (16, 32, DN_CHUNK) if c >= t)
        nbb = _tile(nb, 8)
        t_in, t_cmp, t_valid = t, chunk, t
    nt = t // t_in
    rows = lambda n: pl.BlockSpec((nbb, t_in, n), lambda b, j: (b, j, 0))
    full = lambda a: pl.BlockSpec(a.shape, lambda b, j: (0,) * a.ndim)
    kern = functools.partial(_gdn_kernel, nbb=nbb, t_in=t_in, t_cmp=t_cmp, t_valid=t_valid,
                             pre=pre, chunk=chunk)
    ins = [qkv.reshape(nb, t, 1536), z.reshape(nb, t, 512), ba.reshape(nb, t, 128), conv0, s0,
           cw, alog, dtb, gout, s_stack]
    in_specs = [rows(1536), rows(512), rows(128),
                pl.BlockSpec((None, nbb, 3, 1536), lambda b, j: (layer, b, 0, 0)),
                pl.BlockSpec((None, nbb, 4, 128, 128), lambda b, j: (layer, b, 0, 0, 0)),
                full(cw), full(alog), full(dtb), full(gout), pl.BlockSpec(memory_space=pl.ANY)]
    ao, s_new = pl.pallas_call(
        kern,
        out_shape=[jax.ShapeDtypeStruct((nb, t, 512), BF16 if t_in % 16 == 0 else F32),
                   jax.ShapeDtypeStruct(s_stack.shape, F32)],
        grid=(nb // nbb, nt),
        in_specs=in_specs,
        out_specs=[rows(512), pl.BlockSpec((None, nbb, 4, 128, 128),
                                           lambda b, j: (out_layer, b, 0, 0, 0))],
        scratch_shapes=[pltpu.VMEM((nbb, 8 + t_cmp, 1536), F32),
                        pltpu.VMEM((nbb, 4, 128, 128), F32)],
        input_output_aliases={len(ins) - 1: 1},
        compiler_params=_cp(("parallel", "arbitrary")),
        name="gdn",
    )(*ins)
    return ao.reshape(nb * t, 512), s_new


def _causal_sweep(i, tq, tk, nheads, step):
    assert tk in (tq, 2 * tq)
    n_big = (i * tq) // tk

    def big(j, carry):
        step(pl.multiple_of(j * tk, tk), tk, None)
        return carry

    lax.fori_loop(0, n_big, big, 0)
    if tk != tq:
        @pl.when(i * tq != n_big * tk)
        def _():
            step(pl.multiple_of(n_big * tk, tq), tq, None)

    mask = (lax.broadcasted_iota(jnp.int32, (tq, tq), 1)
            <= lax.broadcasted_iota(jnp.int32, (tq, tq), 0))
    step(pl.multiple_of(i * tq, tq), tq, jnp.concatenate([mask] * nheads, axis=0))


def _online_softmax(s, m_scr, tk):
    m_prev = m_scr[...]
    m_next = jnp.maximum(m_prev, jnp.max(s, axis=-1, keepdims=True))
    p = jnp.exp2(s - _rep(m_next, tk // LANES)).astype(BF16)
    alpha = jnp.exp2(m_prev - m_next)
    m_scr[...] = m_next
    return p, alpha


def _mla_attn_kernel(qlat_ref, qrope_ref, kp_ref, kpt_ref, wuv_ref, o_ref, q_scr, m_scr, acc_scr,
                     *, tq, tk):
    i = pl.program_id(1)
    qr = qrope_ref[...]
    lane = lax.broadcasted_iota(jnp.int32, qr.shape, 1)
    for h in range(4):
        q_scr[h * tq:(h + 1) * tq, 0:128] = qlat_ref[:, h * 128:(h + 1) * 128]
        q_scr[h * tq:(h + 1) * tq, 128:256] = jnp.where(_lane_group(lane, 32, h), qr,
                                                        jnp.zeros_like(qr))
    m_scr[...] = jnp.full(m_scr.shape, -jnp.inf, F32)
    acc_scr[...] = jnp.zeros(acc_scr.shape, F32)

    def step(start, width, mask):
        s = jnp.dot(q_scr[...], kpt_ref[:, pl.ds(start, width)], preferred_element_type=F32)
        if mask is not None:
            s = jnp.where(mask, s, NEG)
        p, alpha = _online_softmax(s, m_scr, width)
        vext = jnp.concatenate([kp_ref[pl.ds(start, width), 0:128],
                                jnp.ones((width, LANES), BF16)], axis=1)
        acc_scr[...] = acc_scr[...] * _rep(alpha, 2) + jnp.dot(
            p, vext, preferred_element_type=F32)

    _causal_sweep(i, tq, tk, 4, step)
    o = (acc_scr[:, 0:128] / acc_scr[:, 128:256]).astype(BF16)
    bo = jnp.dot(o[0:tq], wuv_ref[0:128, :], preferred_element_type=F32)
    for h in range(1, 4):
        bo = bo + jnp.dot(o[h * tq:(h + 1) * tq], wuv_ref[h * 128:(h + 1) * 128, :],
                          preferred_element_type=F32)
    o_ref[...] = bo.astype(BF16)


def _mla_attn(qlat, qrope, kp, kpt, wuv, *, nb, t):
    tq = _tile(t, 512)
    tk = _tile(t, 512)
    nq = t // tq
    kern = functools.partial(_mla_attn_kernel, tq=tq, tk=tk)
    return pl.pallas_call(
        kern,
        out_shape=jax.ShapeDtypeStruct((nb * t, 256), BF16),
        grid=(nb, nq),
        in_specs=[pl.BlockSpec((tq, 512), lambda b, i: (b * nq + i, 0)),
                  pl.BlockSpec((tq, 128), lambda b, i: (b * nq + i, 0)),
                  pl.BlockSpec((t, 256), lambda b, i: (b, 0)),
                  pl.BlockSpec((None, 256, t), lambda b, i: (b, 0, 0)),
                  pl.BlockSpec(wuv.shape, lambda b, i: (0, 0))],
        out_specs=pl.BlockSpec((tq, 256), lambda b, i: (b * nq + i, 0)),
        scratch_shapes=[pltpu.VMEM((4 * tq, 256), BF16), pltpu.VMEM((4 * tq, LANES), F32),
                        pltpu.VMEM((4 * tq, 256), F32)],
        compiler_params=_cp(("parallel", "parallel")),
        name="mla_attn",
    )(qlat, qrope, kp, kpt, wuv)


def _diff_lambda(lq1, lk1, lq2, lk2, lam_init):
    return (jnp.exp(jnp.sum(lq1 * lk1, axis=-1, keepdims=True))
            - jnp.exp(jnp.sum(lq2 * lk2, axis=-1, keepdims=True)) + lam_init)


def _diff_attn_kernel(q_ref, kt_ref, v_ref, lq1_ref, lk1_ref, lq2_ref, lk2_ref, gsub_ref, o_ref,
                      q_scr, m_scr, acc_scr, *, tq, tk, lam_init):
    i = pl.program_id(1)
    q = q_ref[...]
    lane = lax.broadcasted_iota(jnp.int32, q.shape, 1)
    for h in range(8):
        q_scr[h * tq:(h + 1) * tq, :] = jnp.where(_lane_group(lane, 32, h), q, jnp.zeros_like(q))
    m_scr[...] = jnp.full(m_scr.shape, -jnp.inf, F32)
    acc_scr[...] = jnp.zeros(acc_scr.shape, F32)

    def step(start, width, mask):
        s = jnp.dot(q_scr[...], kt_ref[:, pl.ds(start, width)], preferred_element_type=F32)
        if mask is not None:
            s = jnp.where(mask, s, NEG)
        p, alpha = _online_softmax(s, m_scr, width)
        alpha = _rep(alpha, 2)
        ones = jnp.ones((width, LANES), BF16)
        for c in range(4):
            vext = jnp.concatenate([v_ref[pl.ds(start, width), c * 128:(c + 1) * 128], ones],
                                   axis=1)
            r = slice(2 * c * tq, (2 * c + 2) * tq)
            acc_scr[r, :] = acc_scr[r, :] * alpha[r, :] + jnp.dot(
                p[r, :], vext, preferred_element_type=F32)

    _causal_sweep(i, tq, tk, 8, step)
    lam = _diff_lambda(lq1_ref[...], lk1_ref[...], lq2_ref[...], lk2_ref[...], lam_init)
    on = acc_scr[:, 0:128] / acc_scr[:, 128:256]
    lane128 = lax.broadcasted_iota(jnp.int32, (tq, LANES), 1)
    outs = []
    for c in range(4):
        d = on[2 * c * tq:(2 * c + 1) * tq] - lam * on[(2 * c + 1) * tq:(2 * c + 2) * tq]
        outs.append(_rms(d, gsub_ref[...]) * (1.0 - lam_init))
    for j in range(2):
        o_ref[:, j * 128:(j + 1) * 128] = jnp.where(
            lane128 < 64, outs[2 * j], outs[2 * j + 1]).astype(BF16)


def _diff_attn(q, kt, vdup, lq1, lk1, lq2, lk2, gsub, *, nb, t, lam_init):
    tq = _tile(t, 256)
    tk = _tile(t, 512)
    nq = t // tq
    kern = functools.partial(_diff_attn_kernel, tq=tq, tk=tk, lam_init=lam_init)
    vec = lambda a: pl.BlockSpec(a.shape, lambda b, i: (0, 0))
    return pl.pallas_call(
        kern,
        out_shape=jax.ShapeDtypeStruct((nb * t, 256), BF16),
        grid=(nb, nq),
        in_specs=[pl.BlockSpec((tq, 256), lambda b, i: (b * nq + i, 0)),
                  pl.BlockSpec((None, 256, t), lambda b, i: (b, 0, 0)),
                  pl.BlockSpec((t, 512), lambda b, i: (b, 0)),
                  vec(lq1), vec(lk1), vec(lq2), vec(lk2), vec(gsub)],
        out_specs=pl.BlockSpec((tq, 256), lambda b, i: (b * nq + i, 0)),
        scratch_shapes=[pltpu.VMEM((8 * tq, 256), BF16), pltpu.VMEM((8 * tq, LANES), F32),
                        pltpu.VMEM((8 * tq, 256), F32)],
        compiler_params=_cp(("parallel", "parallel")),
        name="diff_attn",
    )(q, kt, vdup, lq1, lk1, lq2, lk2, gsub)


def _paged_kernel(pt_ref, mla_hbm, dk_hbm, dv_hbm, qlat_ref, qr_ref, dq_ref, row_ref, dkn_ref,
                  dvn_ref, wuv_ref, lq1_ref, lk1_ref, lq2_ref, lk2_ref, gsub_ref, bo_ref, co_ref,
                  mla_buf, dk_buf, dv_buf, sem, dqb_scr, ma_scr, la_scr, aa_scr, md_scr, ld_scr,
                  ad_scr, *, g, nslot, ts, layer, lam_init):
    b = pl.program_id(0)
    nb = pl.num_programs(0)

    def copies(seq, slot):
        out = []
        for gi in range(g):
            pg = pt_ref[seq, slot * g + gi]
            for hbm, buf in ((mla_hbm, mla_buf), (dk_hbm, dk_buf), (dv_hbm, dv_buf)):
                out.append(pltpu.make_async_copy(hbm.at[layer, pg], buf.at[slot, gi],
                                                 sem.at[slot]))
        return out

    last = b == nb - 1
    nxt_b = jnp.where(last, b, b + 1)

    @pl.when(b == 0)
    def _():
        for k in range(nslot):
            for c in copies(b, k):
                c.start()

    q = dq_ref[...]
    lane = lax.broadcasted_iota(jnp.int32, q.shape, 1)
    for h in range(8):
        dqb_scr[h * ts:(h + 1) * ts, :] = jnp.where(_lane_group(lane, 32, h), q, 0.0)
    ma_scr[...] = jnp.full(ma_scr.shape, -jnp.inf, F32)
    la_scr[...] = jnp.zeros(la_scr.shape, F32)
    aa_scr[...] = jnp.zeros(aa_scr.shape, F32)
    md_scr[...] = jnp.full(md_scr.shape, -jnp.inf, F32)
    ld_scr[...] = jnp.zeros(ld_scr.shape, F32)
    ad_scr[...] = jnp.zeros(ad_scr.shape, F32)

    qlat = qlat_ref[...].astype(BF16)
    qr = qr_ref[...].astype(BF16)
    dqb = dqb_scr[...].astype(BF16)

    def weights(s_list, m_scr, l_scr):
        s = jnp.concatenate(s_list, axis=-1) if len(s_list) > 1 else s_list[0]
        m_old = m_scr[...]
        m_new = jnp.maximum(m_old, jnp.max(s, axis=-1, keepdims=True))
        p = jnp.exp2(s - m_new)
        corr = jnp.exp2(m_old - m_new)
        l_scr[...] = l_scr[...] * corr + jnp.sum(p, axis=-1, keepdims=True)
        m_scr[...] = m_new
        return p.astype(BF16), corr

    def accumulate(pb, corr, widths, pv, acc_scr):
        acc = acc_scr[...] * corr
        off = 0
        for idx, n in enumerate(widths):
            acc = acc + pv(pb[:, off:off + n], idx)
            off += n
        acc_scr[...] = acc

    def update(s_list, pv, m_scr, l_scr, acc_scr):
        pb, corr = weights(s_list, m_scr, l_scr)
        accumulate(pb, corr, [sb.shape[1] for sb in s_list], pv, acc_scr)

    def consume(slot):
        kts = [mla_buf[slot, gi].astype(BF16) for gi in range(g)]
        sa = [jnp.dot(qlat, kt[0:128, :], preferred_element_type=F32)
              + jnp.dot(qr, kt[128:160, :], preferred_element_type=F32) for kt in kts]
        sd = [jnp.dot(dqb, dk_buf[slot, gi].astype(BF16), preferred_element_type=F32)
              for gi in range(g)]
        pa, ca = weights(sa, ma_scr, la_scr)
        pd, cd = weights(sd, md_scr, ld_scr)
        accumulate(pa, ca, [PAGE] * g, lambda pb, idx: _mm_nt(pb, kts[idx][0:128, :]), aa_scr)
        accumulate(pd, cd, [PAGE] * g, lambda pb, idx: _mm_nt(pb, dv_buf[slot, idx]), ad_scr)

    for k in range(nslot):
        for c in copies(b, k):
            c.wait()
        consume(k)
        for c in copies(nxt_b, k):
            c.start()

    @pl.when(last)
    def _():
        for k in range(nslot):
            for c in copies(nxt_b, k):
                c.wait()

    def finalize():
        tsp = -(-ts // 16) * 16
        qpos = lax.broadcasted_iota(jnp.int32, (ts, tsp), 0)
        kpos = lax.broadcasted_iota(jnp.int32, (ts, tsp), 1)
        mask = kpos <= qpos

        def padded(ref):
            rows = ref[...]
            return jnp.concatenate([rows, jnp.zeros((tsp - ts, rows.shape[1]), F32)],
                                   axis=0).astype(BF16)

        rows = padded(row_ref)
        s = _mm_nt(qlat, rows[:, 0:128]) + _mm_nt(qr, rows[:, 128:160])
        update([jnp.where(jnp.concatenate([mask] * 4, axis=0), s, NEG)],
               lambda pb, idx: jnp.dot(pb, rows[:, 0:128], preferred_element_type=F32),
               ma_scr, la_scr, aa_scr)
        s = _mm_nt(dqb, padded(dkn_ref))
        vn = padded(dvn_ref)
        update([jnp.where(jnp.concatenate([mask] * 8, axis=0), s, NEG)],
               lambda pb, idx: jnp.dot(pb, vn, preferred_element_type=F32),
               md_scr, ld_scr, ad_scr)
        o = (aa_scr[...] / la_scr[...]).astype(BF16)
        bo = jnp.zeros((ts, 256), F32)
        for h in range(4):
            full = jnp.dot(o, wuv_ref[h * 128:(h + 1) * 128, :], preferred_element_type=F32)
            bo = bo + full[h * ts:(h + 1) * ts]
        bo_ref[...] = bo
        on = ad_scr[...] / ld_scr[...]
        lam = _diff_lambda(lq1_ref[...], lk1_ref[...], lq2_ref[...], lk2_ref[...], lam_init)
        lane = lax.broadcasted_iota(jnp.int32, (ts, 256), 1)
        d = jnp.zeros((ts, 256), F32)
        for c in range(4):
            dc = on[2 * c * ts:(2 * c + 1) * ts] - lam * on[(2 * c + 1) * ts:(2 * c + 2) * ts]
            d = jnp.where(_lane_group(lane, 64, c), dc, d)
        ms = jnp.zeros((ts, 256), F32)
        for c in range(4):
            sel = _lane_group(lane, 64, c)
            ssq = jnp.sum(jnp.where(sel, d * d, 0.0), axis=-1, keepdims=True) * (1.0 / 64)
            ms = jnp.where(sel, ssq, ms)
        co_ref[...] = d * lax.rsqrt(ms + EPS) * gsub_ref[...] * (1.0 - lam_init)

    finalize()


def _paged(page_table, cache_mla_t, cache_dk_t, cache_dv_t, layer, qlat, qr, dq, row, dkn, dvn,
           wuv, lq1, lk1, lq2, lk2, gsub4, *, nb, ts, lam_init):
    n_pages = page_table.shape[1]
    g = _tile(n_pages, 16)
    nslot = n_pages // g
    kern = functools.partial(_paged_kernel, g=g, nslot=nslot, ts=ts, layer=layer,
                             lam_init=lam_init)
    per_b = lambda r, n: pl.BlockSpec((None, r, n), lambda b, pt: (b, 0, 0))
    vec = lambda a: pl.BlockSpec(a.shape, lambda b, pt: (0, 0))
    hbm = pl.BlockSpec(memory_space=pl.ANY)
    in_specs = [hbm, hbm, hbm,
                per_b(4 * ts, 128), per_b(4 * ts, 32), per_b(ts, 256), per_b(ts, 160),
                per_b(ts, 256), per_b(ts, 256), vec(wuv), vec(lq1), vec(lk1), vec(lq2),
                vec(lk2), vec(gsub4)]
    grid_spec = pltpu.PrefetchScalarGridSpec(
        num_scalar_prefetch=1,
        grid=(nb,),
        in_specs=in_specs,
        out_specs=[per_b(ts, 256), per_b(ts, 256)],
        scratch_shapes=[pltpu.VMEM((nslot, g, 160, PAGE), F32),
                        pltpu.VMEM((nslot, g, 256, PAGE), F32),
                        pltpu.VMEM((nslot, g, 256, PAGE), F32),
                        pltpu.SemaphoreType.DMA((nslot,)),
                        pltpu.VMEM((8 * ts, 256), F32),
                        pltpu.VMEM((4 * ts, 1), F32), pltpu.VMEM((4 * ts, 1), F32),
                        pltpu.VMEM((4 * ts, 128), F32),
                        pltpu.VMEM((8 * ts, 1), F32), pltpu.VMEM((8 * ts, 1), F32),
                        pltpu.VMEM((8 * ts, 256), F32)])
    return pl.pallas_call(
        kern,
        out_shape=[jax.ShapeDtypeStruct((nb, ts, 256), F32),
                   jax.ShapeDtypeStruct((nb, ts, 256), F32)],
        grid_spec=grid_spec,
        compiler_params=_cp(("arbitrary",)),
        name="paged_attn",
    )(page_table, cache_mla_t, cache_dk_t, cache_dv_t,
      qlat, qr, dq, row, dkn, dvn, wuv, lq1, lk1, lq2, lk2, gsub4)


def _pack_w_in(w):
    d = w.shape[0]
    a_qkv, a_z, a_b, a_a, b_cq, b_ckv, b_kr, c_q, c_k, c_v = jnp.split(
        w, np.cumsum([1536, 512, 4, 4, 192, 128, 32, 256, 256]).tolist(), axis=1)
    z = lambda n: jnp.zeros((d, n), w.dtype)
    kr_sw = jnp.concatenate([-b_kr[:, 16:32], b_kr[:, 0:16]], axis=1)
    v4 = c_v.reshape(d, 4, 64)
    vdup = jnp.concatenate([v4, v4], axis=2).reshape(d, 512)
    return jnp.concatenate(
        [a_qkv, a_z, a_b, a_a, z(120), b_ckv, b_cq, z(64), jnp.tile(b_kr, (1, 4)),
         jnp.tile(kr_sw, (1, 4)), c_q, c_k, vdup], axis=1).astype(BF16)


def _pack_mla(w_uq, w_uk, w_uv):
    nope = w_uq[:, :, 0:64].reshape(192, 256)
    rope = w_uq[:, :, 64:96]
    rope_sw = jnp.concatenate([-rope[:, :, 16:32], rope[:, :, 0:16]], axis=2)
    wuq = jnp.concatenate([nope, rope.reshape(192, 128), rope_sw.reshape(192, 128)], axis=1)
    wuq = jnp.concatenate([wuq, jnp.zeros((64, 512), wuq.dtype)], axis=0)
    eye = jnp.eye(4, dtype=w_uk.dtype)
    wuk_bd = jnp.einsum('rhn,hg->hngr', w_uk, eye).reshape(256, 512)
    wuv_bd = jnp.einsum('rhe,hg->hrge', w_uv, eye).reshape(512, 256)
    return wuq.astype(BF16), wuk_bd.astype(BF16), wuv_bd.astype(BF16)


def _rope_tables(pos):
    freqs = jnp.power(ROPE_THETA, -jnp.arange(16, dtype=F32) / 16)
    ang = pos[:, None] * freqs[None, :]
    return jnp.tile(jnp.cos(ang), (1, 8)), jnp.tile(jnp.sin(ang), (1, 8))


def _lane_pad(v, n=LANES, offset=0):
    v = v.reshape(1, -1).astype(F32)
    return jnp.pad(v, ((0, 0), (offset, n - offset - v.shape[1])))


def kernel(x_prompt, x_sample, cache_mla, cache_diff_k, cache_diff_v, state_dn_S, state_dn_conv, page_table, g_f1_pre, g_f1_post, w_f1_gate, w_f1_up, w_f1_down, g_mix_pre, g_mix_post, w_in, w_out, dn_conv_w, dn_A_log, dn_dt_bias, dn_g_out, mla_g_q, mla_w_uq, mla_g_kv, mla_w_uk, mla_w_uv, diff_lam_q1, diff_lam_k1, diff_lam_q2, diff_lam_k2, diff_g_subln, g_f2_pre, g_f2_post, w_f2_gate, w_f2_up, w_f2_down):
    bp, tp, d = x_prompt.shape
    bs, ts, _ = x_sample.shape
    depth = w_in.shape[0]
    n_pages = page_table.shape[1]
    past_len = n_pages * PAGE
    mla_qscale = 96.0 ** -0.5 * LOG2E
    diff_qscale = 32.0 ** -0.5 * LOG2E

    cos_p, sin_p = _rope_tables(jnp.arange(tp, dtype=F32))
    cos_p, sin_p = jnp.tile(cos_p, (bp, 1)), jnp.tile(sin_p, (bp, 1))
    cos_s, sin_s = _rope_tables(past_len + jnp.arange(ts, dtype=F32))
    cos_s, sin_s = jnp.tile(cos_s, (bs, 1)), jnp.tile(sin_s, (bs, 1))

    n_pool = cache_mla.shape[1]
    cache_mla_t = jnp.swapaxes(cache_mla, 2, 3)
    cache_dk_t = jnp.transpose(cache_diff_k, (0, 1, 3, 4, 2)).reshape(depth, n_pool, 256, PAGE)
    cache_dv_t = jnp.transpose(cache_diff_v, (0, 1, 3, 4, 2)).reshape(depth, n_pool, 256, PAGE)
    zero_conv = jnp.zeros((1, bp, CONV_W - 1, 1536), F32)
    zero_s = jnp.zeros((1, bp, 4, 128, 128), F32)

    xp = x_prompt.reshape(bp * tp, d)
    xs = x_sample.reshape(bs * ts, d)
    p_conv = []
    p_fm = [jnp.zeros((depth, bp, n, tp), F32) for n in (160, 256, 256)]
    p_s = jnp.zeros((depth, bp, 4, 128, 128), F32)
    s_s = jnp.zeros((depth, bs, 4, 128, 128), F32)
    s_out = [[] for _ in range(4)]
    row1 = lambda v: v.reshape(1, -1).astype(F32)
    for l in range(depth):
        lam_init = 0.8 - 0.6 * math.exp(-0.3 * l)
        wg1, wu1, wd1 = (w_f1_gate[l].astype(BF16), w_f1_up[l].astype(BF16),
                         w_f1_down[l].astype(BF16))
        wg2, wu2, wd2 = (w_f2_gate[l].astype(BF16), w_f2_up[l].astype(BF16),
                         w_f2_down[l].astype(BF16))
        w_inp = _pack_w_in(w_in[l])
        wuq, wuk_bd, wuv_bd = _pack_mla(mla_w_uq[l], mla_w_uk[l], mla_w_uv[l])
        w_o = w_out[l].astype(BF16)
        gq = _lane_pad(mla_g_q[l], 256)
        gkv = row1(mla_g_kv[l])
        alog = _lane_pad(dn_A_log[l], offset=4)
        dtb = _lane_pad(dn_dt_bias[l], offset=4)
        gout = row1(dn_g_out[l])
        cw = dn_conv_w[l].astype(F32)
        lq1, lk1, lq2, lk2 = (_lane_pad(diff_lam_q1[l]), _lane_pad(diff_lam_k1[l]),
                              _lane_pad(diff_lam_q2[l]), _lane_pad(diff_lam_k2[l]))
        gsub2 = jnp.tile(row1(diff_g_subln[l]), (1, 2))
        gsub4 = jnp.tile(row1(diff_g_subln[l]), (1, 4))

        def front(x, cos, sin, seq=None):
            x = _ffn(x, row1(g_f1_pre[l]), wg1, wu1, wd1, row1(g_f1_post[l]))
            return x, _proj(x, row1(g_mix_pre[l]), w_inp, cos, sin, gq, wuq, wuk_bd, gkv,
                            mla_qscale=mla_qscale, diff_qscale=diff_qscale, seq=seq)

        def back(x, ao, bo, co):
            return _ffn(x, row1(g_f2_pre[l]), wg2, wu2, wd2, row1(g_f2_post[l]),
                        mix=(ao, bo, co, w_o, row1(g_mix_post[l])))

        xp, (qkv, z, ba, kp, qlat, qrope, dq, dvb, kpt, dkbt, *p_fm, tail) = front(
            xp, cos_p, sin_p, seq=(l, bp, tp, p_fm, cw))
        ao, p_s = _gdn(qkv, z, ba, zero_conv, zero_s, 0, cw, alog, dtb, gout, p_s, nb=bp, t=tp,
                       out_layer=l, pre=True)
        bo = _mla_attn(qlat, qrope, kp, kpt, wuv_bd, nb=bp, t=tp)
        co = _diff_attn(dq, dkbt, dvb, lq1, lk1, lq2, lk2, gsub2, nb=bp, t=tp, lam_init=lam_init)
        xp = back(xp, ao, bo, co)
        p_conv.append(tail[:, 8 - (CONV_W - 1):])

        xs, (qkv, z, ba, kp, qlat, qrope, dq, dvb, mrow, dk, dv) = front(xs, cos_s, sin_s)
        ao, s_s = _gdn(qkv, z, ba, state_dn_conv, state_dn_S, l, cw, alog, dtb, gout, s_s,
                       nb=bs, t=ts, out_layer=l, pre=False)
        qlat_h = qlat.reshape(bs, ts, 4, 128).transpose(0, 2, 1, 3).reshape(bs, 4 * ts, 128)
        qr_h = qrope.reshape(bs, ts, 4, 32).transpose(0, 2, 1, 3).reshape(bs, 4 * ts, 32)
        bo, co = _paged(page_table, cache_mla_t, cache_dk_t, cache_dv_t, l, qlat_h, qr_h,
                        dq.reshape(bs, ts, 256).astype(F32), mrow.reshape(bs, ts, 160),
                        dk.reshape(bs, ts, 256), dv.reshape(bs, ts, 256), wuv_bd,
                        lq1, lk1, lq2, lk2, gsub4, nb=bs, ts=ts, lam_init=lam_init)
        xs = back(xs, ao, bo.reshape(bs * ts, 256), co.reshape(bs * ts, 256))
        s_out[0].append(qkv.reshape(bs, ts, 1536)[:, ts - (CONV_W - 1):])
        s_out[1].append(mrow.reshape(bs, ts, 160))
        s_out[2].append(dk.reshape(bs, ts, 8, 32))
        s_out[3].append(dv.reshape(bs, ts, 4, 64))

    rowt, dkt, dvt = p_fm
    p_mla = jnp.swapaxes(rowt, 2, 3)
    p_dk = jnp.transpose(dkt.reshape(depth, bp, 8, 32, tp), (0, 1, 4, 2, 3))
    p_dv = jnp.transpose(dvt.reshape(depth, bp, 4, 64, tp), (0, 1, 4, 2, 3))
    s_conv, s_mla, s_dk, s_dv = [jnp.stack(a, axis=0) for a in s_out]
    return (xp.reshape(bp, tp, d), xs.reshape(bs, ts, d), p_mla, p_dk, p_dv, p_s,
            jnp.stack(p_conv, axis=0), s_mla, s_dk, s_dv, s_s, s_conv)
```
